```python
import jax, jax.numpy as jnp
from jax import lax
import numpy as np

D_MODEL = 1024
BATCH = 8
SEQ = 2048
DEPTH = 2
DEC_BATCH = 8
DEC_SEQ = 64
PAST_LEN = 2048

CHUNK = 64
EPS = 1e-6
N_BRANCH = 3
POOL_WINDOWS = (2, 4, 8, 16)
POOL_GROUPS = 4
POOL_GW = D_MODEL // 8
D_A = POOL_GROUPS * POOL_GW
POOL_HIST = 16 - 1
SGU_LEN = 128
SGU_GROUPS = 4
D_B = D_MODEL // 2
SGU_GW = D_B // SGU_GROUPS
GLA_HEADS = 4
GLA_DK = D_MODEL // 2 // GLA_HEADS
GLA_DV = D_MODEL // GLA_HEADS
D_CK = GLA_HEADS * GLA_DK
D_CV = GLA_HEADS * GLA_DV
GLA_RANK = 16
GLA_NORMALIZER = 16.0
GLA_BLOCK = CHUNK // 4
SPLIT_SIZES = (D_A, D_A, D_B, D_B, D_B, D_CK, D_CK, D_CV, D_CV, GLA_RANK, N_BRANCH * D_MODEL)
D_IN = 2 * D_A + 3 * D_B + 2 * D_CK + 2 * D_CV + GLA_RANK + N_BRANCH * D_MODEL

kernel_name = "hybrid_pool_sgu_gla_streaming_step"


def rmsnorm(x, g):
    xf = x.astype(jnp.float32)
    y = xf * lax.rsqrt(jnp.mean(xf * xf, axis=-1, keepdims=True) + EPS)
    return (y * g.astype(jnp.float32)).astype(x.dtype)


def layernorm(x, g):
    xf = x.astype(jnp.float32)
    mu = jnp.mean(xf, axis=-1, keepdims=True)
    xc = xf - mu
    y = xc * lax.rsqrt(jnp.mean(xc * xc, axis=-1, keepdims=True) + EPS)
    return (y * g.astype(jnp.float32)).astype(x.dtype)


def split_cols(z):
    idx = [int(i) for i in np.cumsum(SPLIT_SIZES)[:-1]]
    return jnp.split(z, idx, axis=-1)


def ada_modulation(c, w, b):
    mod = (jax.nn.silu(c) @ w + b)[:, None, :]
    shift, scale, gate = jnp.split(mod, 3, axis=-1)
    return shift, scale, gate


def pool_mixer(a, hist, pos0, pool_w, pool_scale):
    B, T, _ = a.shape
    full = jnp.concatenate([hist, a], axis=1)
    ff = full.astype(jnp.float32)
    cs = jnp.concatenate([jnp.zeros((B, 1, D_A), jnp.float32), jnp.cumsum(ff, axis=1)], axis=1)
    end = cs[:, POOL_HIST + 1:]
    pos = pos0 + jnp.arange(T)
    means = []
    for gi, w in enumerate(POOL_WINDOWS):
        sl = slice(gi * POOL_GW, (gi + 1) * POOL_GW)
        start = cs[:, POOL_HIST + 1 - w: POOL_HIST + 1 - w + T, sl]
        cnt = jnp.minimum(w, pos + 1).astype(jnp.float32)[None, :, None]
        means.append((end[..., sl] - start) / cnt)
    d = (jnp.concatenate(means, axis=-1) - ff[:, POOL_HIST:]).astype(a.dtype)
    d = d.reshape(B, T, POOL_GROUPS, POOL_GW)
    y = jnp.einsum('btgc,gcd->btgd', d, pool_w).reshape(B, T, D_A) * pool_scale
    new_hist = full[:, -POOL_HIST:]
    return y, new_hist


def sgu_mixer(u, v, norm_g, w_s, b_s):
    B, T, _ = v.shape
    L = min(T, SGU_LEN)
    nc = T // L
    vn = layernorm(v, norm_g)
    vr = vn.reshape(B, nc, L, SGU_GROUPS, SGU_GW)
    wm = jnp.tril(w_s[:, :L, :L])
    bias = b_s[:, :L].T[None, None, :, :, None]
    s = jnp.einsum('gij,bnjgc->bnigc', wm, vr) + bias
    return u * s.reshape(B, T, D_B), vn


def gla_scan(q, k, v, log_a, s0):
    B, T = q.shape[0], q.shape[1]
    pad = (-T) % GLA_BLOCK
    padt = lambda t: jnp.pad(t.astype(jnp.float32), ((0, 0), (0, pad), (0, 0), (0, 0)))

    def to_blocks(t):
        Tp, H, X = t.shape[1], t.shape[2], t.shape[3]
        return t.reshape(B, Tp // GLA_BLOCK, GLA_BLOCK, H, X).transpose(1, 0, 3, 2, 4)

    qb, kb, vb, gb = (to_blocks(padt(t)) for t in (q, k, v, log_a))
    mask = jnp.tril(jnp.ones((GLA_BLOCK, GLA_BLOCK), bool))[:, :, None]

    def step(S, inp):
        qi, ki, vi, gi = inp
        bc = jnp.cumsum(gi, axis=2)
        diff = bc[:, :, :, None, :] - bc[:, :, None, :, :]
        decay = jnp.where(mask, jnp.exp(jnp.where(mask, diff, 0.0)), 0.0)
        att = jnp.einsum('bhid,bhjd,bhijd->bhij', qi, ki, decay)
        o = jnp.einsum('bhij,bhjv->bhiv', att, vi) + jnp.einsum('bhid,bhdv->bhiv', qi * jnp.exp(bc), S)
        blast = bc[:, :, -1]
        kdec = ki * jnp.exp(blast[:, :, None, :] - bc)
        S = jnp.exp(blast)[..., None] * S + jnp.einsum('bhjd,bhjv->bhdv', kdec, vi)
        return S, o

    s_fin, ob = lax.scan(step, s0.astype(jnp.float32), (qb, kb, vb, gb))
    o = ob.transpose(1, 0, 3, 2, 4).reshape(B, T + pad, GLA_HEADS, GLA_DV)[:, :T]
    return o.astype(v.dtype), s_fin.astype(s0.dtype)


def mixer_layer(x, c, pool_hist, gla_state, pos0, lp):
    (ada_w, ada_b, pre_g, post_g, w_in, pool_w, pool_scale, sgu_g, sgu_w, sgu_b,
     wa2, ba, gla_g, w_oa, w_ob, w_oc, w_out) = lp
    B, T, _ = x.shape
    shift, scale, gate = ada_modulation(c, ada_w, ada_b)
    h = rmsnorm(x, pre_g) * (1.0 + scale) + shift
    z = h @ w_in
    a, g_a, u, v_b, g_b, q, k, v_c, g_c, z_lr, g_m = split_cols(z)
    y_a, new_hist = pool_mixer(a, pool_hist, pos0, pool_w, pool_scale)
    y_a = y_a * jax.nn.silu(g_a)
    y_b, v_rows = sgu_mixer(u, v_b, sgu_g, sgu_w, sgu_b)
    y_b = y_b * jax.nn.silu(g_b)
    log_a = jax.nn.log_sigmoid((z_lr @ wa2 + ba).astype(jnp.float32)) / GLA_NORMALIZER
    heads = lambda t, d: t.reshape(B, T, GLA_HEADS, d)
    o, new_state = gla_scan(heads(q, GLA_DK) * (GLA_DK ** -0.5), heads(k, GLA_DK),
                            heads(v_c, GLA_DV), heads(log_a, GLA_DK), gla_state)
    y_c = rmsnorm(o, gla_g).reshape(B, T, D_CV) * jax.nn.silu(g_c)
    gm = jax.nn.sigmoid(g_m)
    m = (gm[..., :D_MODEL] * (y_a @ w_oa)
         + gm[..., D_MODEL:2 * D_MODEL] * (y_b @ w_ob)
         + gm[..., 2 * D_MODEL:] * (y_c @ w_oc))
    out = m @ w_out
    x = x + gate * rmsnorm(out, post_g)
    return x, new_hist, new_state, v_rows


def setup_inputs(seed: int = 0) -> dict:
    key = jax.random.key(seed)
    ks = jax.random.split(key, 26)
    nrm = lambda k, shape, s: jax.random.normal(k, shape, jnp.float32) * s
    D = D_MODEL
    return {
        "x_prompt": nrm(ks[0], (BATCH, SEQ, D), 1.0),
        "x_sample": nrm(ks[1], (DEC_BATCH, DEC_SEQ, D), 1.0),
        "state_pool": nrm(ks[2], (DEPTH, DEC_BATCH, POOL_HIST, D_A), 1.0),
        "state_gla": nrm(ks[3], (DEPTH, DEC_BATCH, GLA_HEADS, GLA_DK, GLA_DV), 1.0),
        "c_prompt": nrm(ks[4], (BATCH, D), 1.0),
        "c_sample": nrm(ks[5], (DEC_BATCH, D), 1.0),
        "ada_w": nrm(ks[6], (DEPTH, D, 3 * D), 0.3 * D ** -0.5),
        "ada_b": nrm(ks[7], (DEPTH, 3 * D), 0.1),
        "pre_norm_g": 1.0 + nrm(ks[8], (DEPTH, D), 0.05),
        "post_norm_g": 1.0 + nrm(ks[9], (DEPTH, D), 0.05),
        "w_in": nrm(ks[10], (DEPTH, D, D_IN), D ** -0.5),
        "pool_w": nrm(ks[11], (DEPTH, POOL_GROUPS, POOL_GW, POOL_GW), POOL_GW ** -0.5),
        "pool_scale": 1.0 + nrm(ks[12], (DEPTH, D_A), 0.1),
        "sgu_norm_g": 1.0 + nrm(ks[13], (DEPTH, D_B), 0.05),
        "sgu_w": nrm(ks[14], (DEPTH, SGU_GROUPS, SGU_LEN, SGU_LEN), SGU_LEN ** -0.5),
        "sgu_b": 1.0 + nrm(ks[15], (DEPTH, SGU_GROUPS, SGU_LEN), 0.1),
        "gla_wa2": nrm(ks[16], (DEPTH, GLA_RANK, D_CK), GLA_RANK ** -0.5),
        "gla_ba": nrm(ks[17], (DEPTH, D_CK), 0.1),
        "gla_norm_g": 1.0 + nrm(ks[18], (DEPTH, GLA_DV), 0.05),
        "w_oa": nrm(ks[19], (DEPTH, D_A, D), D_A ** -0.5),
        "w_ob": nrm(ks[20], (DEPTH, D_B, D), D_B ** -0.5),
        "w_oc": nrm(ks[21], (DEPTH, D_CV, D), D_CV ** -0.5),
        "w_out": nrm(ks[22], (DEPTH, D, D), D ** -0.5),
    }


def reference(x_prompt, x_sample, state_pool, state_gla, c_prompt, c_sample,
              ada_w, ada_b, pre_norm_g, post_norm_g, w_in, pool_w, pool_scale,
              sgu_norm_g, sgu_w, sgu_b, gla_wa2, gla_ba, gla_norm_g,
              w_oa, w_ob, w_oc, w_out):
    params = (ada_w, ada_b, pre_norm_g, post_norm_g, w_in, pool_w, pool_scale,
              sgu_norm_g, sgu_w, sgu_b, gla_wa2, gla_ba, gla_norm_g,
              w_oa, w_ob, w_oc, w_out)
    xp, xs = x_prompt, x_sample
    bp = xp.shape[0]
    pool_p, gla_p, pool_s, gla_s, sgu_s = [], [], [], [], []
    for l in range(DEPTH):
        lp = tuple(p[l] for p in params)
        hist0 = jnp.zeros((bp, POOL_HIST, D_A), xp.dtype)
        s0 = jnp.zeros((bp, GLA_HEADS, GLA_DK, GLA_DV), state_gla.dtype)
        xp, hp, sp, _ = mixer_layer(xp, c_prompt, hist0, s0, 0, lp)
        xs, hs, ss, vs = mixer_layer(xs, c_sample, state_pool[l], state_gla[l], PAST_LEN, lp)
        pool_p.append(hp)
        gla_p.append(sp)
        pool_s.append(hs)
        gla_s.append(ss)
        sgu_s.append(vs)
    new_pool_prompt = jnp.stack(pool_p)
    new_gla_prompt = jnp.stack(gla_p)
    new_pool_sample = jnp.stack(pool_s)
    new_gla_sample = jnp.stack(gla_s)
    new_sgu_v_sample = jnp.stack(sgu_s)
    return (xp, xs, new_pool_prompt, new_gla_prompt, new_pool_sample, new_gla_sample, new_sgu_v_sample)
```

```python
import functools

import jax
import jax.numpy as jnp
from jax import lax
from jax.experimental import pallas as pl
from jax.experimental.pallas import tpu as pltpu

D_MODEL = 1024
DEPTH = 2
PAST_LEN = 2048
EPS = 1e-6
POOL_WINDOWS = (2, 4, 8, 16)
POOL_GW = 128
D_A = 512
POOL_HIST = 15
HIST_ROWS = 16
SGU_LEN = 128
SGU_GROUPS = 4
D_B = 512
SGU_GW = 128
GLA_HEADS = 4
GLA_DK = 128
GLA_DV = 256
D_CK = 512
D_CV = 1024
GLA_RANK = 16
GLA_NORMALIZER = 16.0
GLA_SUB = 16
LANE = 128
V7X_VMEM_BYTES = 64 * 1024 * 1024

OFF_A, OFF_GA, OFF_U, OFF_VB, OFF_GB = 0, 512, 1024, 1536, 2048
OFF_Q, OFF_K, OFF_VC, OFF_GC = 2560, 3072, 3584, 4608
D_MAIN = 5632

F32 = jnp.float32
BF16 = jnp.bfloat16


def _dot(a, b):
    return jnp.dot(a, b, preferred_element_type=F32)


def _dot_nt(a, b):
    return lax.dot_general(a, b, (((1,), (1,)), ((), ())), preferred_element_type=F32)


def _dot_tn(a, b):
    return lax.dot_general(a, b, (((0,), (0,)), ((), ())), preferred_element_type=F32)


def _sigmoid(x):
    return 1.0 / (1.0 + jnp.exp(-x))


def _silu(x):
    return x * _sigmoid(x)


def _rms(x, g):
    return x * lax.rsqrt(jnp.mean(x * x, axis=-1, keepdims=True) + EPS) * g


def _ada_kernel(c_ref, w_ref, b_ref, o_ref):
    c = c_ref[...]
    o_ref[0] = _dot(_silu(c), w_ref[0]) + b_ref[0]


def _ada_modulation(c_all, ada_w, ada_b):
    n = c_all.shape[0]
    nt = 3
    return pl.pallas_call(
        _ada_kernel,
        grid=(DEPTH, nt),
        in_specs=[
            pl.BlockSpec((n, D_MODEL), lambda l, j: (0, 0)),
            pl.BlockSpec((1, D_MODEL, D_MODEL), lambda l, j: (l, 0, j)),
            pl.BlockSpec((1, 1, D_MODEL), lambda l, j: (l, 0, j)),
        ],
        out_specs=pl.BlockSpec((1, n, D_MODEL), lambda l, j: (l, 0, j)),
        out_shape=jax.ShapeDtypeStruct((DEPTH, n, 3 * D_MODEL), F32),
        name="ada_modulation",
    )(c_all, ada_w, ada_b.reshape(DEPTH, 1, 3 * D_MODEL))


def _layer_kernel(x_ref, mod_ref, hist0_ref, s0_ref,
                  pre_g_ref, post_g_ref, w_main_ref, w_lr_ref, w_gm_ref,
                  pool_w_ref, pool_scale_ref, sgu_g_ref, sgu_w_ref, sgu_b_ref,
                  wa2_ref, ba_ref, gla_g_ref, w_oa_ref, w_ob_ref, w_oc_ref, w_out_ref,
                  *rest, BB, TT, L, C, pos0, emit_vn):
    if emit_vn:
        y_ref, hist_ref, sfin_ref, vn_ref = rest[:4]
        scratch = rest[4:]
    else:
        y_ref, hist_ref, sfin_ref = rest[:3]
        vn_ref = None
        scratch = rest[3:]
    hb_ref, ext_ref, q_ref, k_ref, v_ref, la_ref, o_ref, st_ref = scratch

    M = BB * TT
    t = pl.program_id(1)
    nt = pl.num_programs(1)

    @pl.when(t == 0)
    def _():
        ext_ref[:, 0:HIST_ROWS, :] = hist0_ref[...]
        for bi in range(BB):
            for h in range(GLA_HEADS):
                st_ref[bi, h] = s0_ref[bi, h].T

    pre_g = pre_g_ref[...]
    for bi in range(BB):
        xb = x_ref[bi]
        shift = mod_ref[bi, 0:1, :]
        scale = mod_ref[bi, 1:2, :]
        hmod = _rms(xb, pre_g) * (1.0 + scale) + shift
        hb_ref[bi * TT:(bi + 1) * TT, :] = hmod.astype(BF16)
    hb = hb_ref[...]

    def proj(off, width):
        return _dot(hb, w_main_ref[:, off:off + width])

    a = proj(OFF_A, D_A)
    row = lax.broadcasted_iota(jnp.int32, (TT, POOL_GW), 0)
    pos1 = row + (t * TT + pos0 + 1)
    y_a_parts = []
    for bi in range(BB):
        ab = a[bi * TT:(bi + 1) * TT]
        ext_ref[bi, HIST_ROWS:HIST_ROWS + TT, :] = ab
        groups = []
        for gi, w in enumerate(POOL_WINDOWS):
            ls = slice(gi * POOL_GW, (gi + 1) * POOL_GW)
            terms = [ext_ref[bi, HIST_ROWS - k:HIST_ROWS - k + TT, ls] for k in range(w)]
            while len(terms) > 1:
                terms = [terms[i] + terms[i + 1] for i in range(0, len(terms), 2)]
            cnt = jnp.minimum(pos1, w).astype(F32)
            dgi = terms[0] / cnt - ab[:, ls]
            groups.append(_dot(dgi.astype(BF16), pool_w_ref[gi]))
        y_a_parts.append(jnp.concatenate(groups, axis=-1))
        tail = ext_ref[bi, TT:TT + HIST_ROWS, :]
        ext_ref[bi, 0:HIST_ROWS, :] = tail
        hist_ref[bi] = tail
    y_a = y_a_parts[0] if BB == 1 else jnp.concatenate(y_a_parts, axis=0)
    g_a = proj(OFF_GA, D_A)
    y_a = y_a * pool_scale_ref[...] * _silu(g_a)
    merged = _sigmoid(_dot(hb, w_gm_ref[:, 0:D_MODEL])) * _dot(y_a.astype(BF16), w_oa_ref[...])

    vb = proj(OFF_VB, D_B)
    mu = jnp.mean(vb, axis=-1, keepdims=True)
    vc = vb - mu
    vn = vc * lax.rsqrt(jnp.mean(vc * vc, axis=-1, keepdims=True) + EPS) * sgu_g_ref[...]
    if emit_vn:
        for bi in range(BB):
            vn_ref[bi] = vn[bi * TT:(bi + 1) * TT]
    vnb = vn.astype(BF16)
    ri = lax.broadcasted_iota(jnp.int32, (L, L), 0)
    ci_ = lax.broadcasted_iota(jnp.int32, (L, L), 1)
    causal = ri >= ci_
    wms = [jnp.where(causal, sgu_w_ref[g], 0.0).astype(BF16) for g in range(SGU_GROUPS)]
    sgu_bias = sgu_b_ref[...]
    s_rows = []
    for c0 in range(0, M, L):
        parts = [_dot(wms[g], vnb[c0:c0 + L, g * SGU_GW:(g + 1) * SGU_GW])
                 for g in range(SGU_GROUPS)]
        s_rows.append(jnp.concatenate(parts, axis=-1) + sgu_bias)
    s_all = s_rows[0] if len(s_rows) == 1 else jnp.concatenate(s_rows, axis=0)
    y_b = proj(OFF_U, D_B) * s_all * _silu(proj(OFF_GB, D_B))
    merged = merged + (_sigmoid(_dot(hb, w_gm_ref[:, D_MODEL:2 * D_MODEL]))
                       * _dot(y_b.astype(BF16), w_ob_ref[...]))

    q_ref[...] = proj(OFF_Q, D_CK) * (GLA_DK ** -0.5)
    k_ref[...] = proj(OFF_K, D_CK)
    v_ref[...] = proj(OFF_VC, D_CV).astype(BF16)
    z_lr = _dot(hb, w_lr_ref[...])
    pre = _dot(z_lr.astype(BF16), wa2_ref[...]) + ba_ref[...]
    log_sig = jnp.minimum(pre, 0.0) - jnp.log1p(jnp.exp(-jnp.abs(pre)))
    la_ref[...] = log_sig / GLA_NORMALIZER

    cr = lax.broadcasted_iota(jnp.int32, (C, C), 0)
    cc = lax.broadcasted_iota(jnp.int32, (C, C), 1)
    tri = (cr >= cc).astype(BF16)
    sub_shift = GLA_SUB.bit_length() - 1
    mask_diag = (cr >= cc) & ((cr >> sub_shift) == (cc >> sub_shift))
    rowi = lax.broadcasted_iota(jnp.int32, (C, GLA_DK), 0)
    levels = []
    s = GLA_SUB
    while 2 * s <= C:
        sh = s.bit_length() - 1
        levels.append((s, ((rowi >> sh) & 1) == 1, (cr >> (sh + 1)) == (cc >> (sh + 1))))
        s *= 2
    chunks_per_seq = TT // C

    def rows_bcast(src, idx, n):
        return jnp.broadcast_to(src[idx:idx + 1, :], (n, src.shape[1]))

    def gla_chunk(ci, carry):
        r0 = pl.multiple_of(ci * C, C)
        bi = 0 if BB == 1 else ci // chunks_per_seq
        la = la_ref[pl.ds(r0, C), :]
        p0 = la.astype(BF16)
        r1 = la - p0.astype(F32)
        p1 = r1.astype(BF16)
        p2 = (r1 - p1.astype(F32)).astype(BF16)
        b_all = _dot(tri, p0) + _dot(tri, p1) + _dot(tri, p2)
        for h in range(GLA_HEADS):
            ks = slice(h * GLA_DK, (h + 1) * GLA_DK)
            vs = slice(h * GLA_DV, (h + 1) * GLA_DV)
            b = b_all[:, ks]
            qh = q_ref[pl.ds(r0, C), ks]
            kh = k_ref[pl.ds(r0, C), ks]
            vh = v_ref[pl.ds(r0, C), vs]
            blast = b[C - 1:C, :]
            q_in = (qh * jnp.exp(b)).astype(BF16)
            k_dec = (kh * jnp.exp(blast - b)).astype(BF16)
            ref_sub = jnp.concatenate(
                [jnp.zeros((GLA_SUB, GLA_DK), F32)]
                + [rows_bcast(b, GLA_SUB * i - 1, GLA_SUB) for i in range(1, C // GLA_SUB)], axis=0)
            qd = (qh * jnp.exp(b - ref_sub)).astype(BF16)
            kd = (kh * jnp.exp(ref_sub - b)).astype(BF16)
            att = jnp.where(mask_diag, _dot_nt(qd, kd), 0.0)
            for (sz, upper, same_blk) in levels:
                ref_mid = jnp.concatenate(
                    [rows_bcast(b, p + sz - 1, 2 * sz) for p in range(0, C, 2 * sz)], axis=0)
                ql = jnp.where(upper, qh * jnp.exp(jnp.minimum(b - ref_mid, 0.0)), 0.0).astype(BF16)
                kl = jnp.where(upper, 0.0, kh * jnp.exp(jnp.minimum(ref_mid - b, 0.0))).astype(BF16)
                att = att + jnp.where(same_blk, _dot_nt(ql, kl), 0.0)
            st = st_ref[bi, h]
            o = _dot(att.astype(BF16), vh) + _dot_nt(q_in, st.astype(BF16))
            o_ref[pl.ds(r0, C), vs] = o
            st_ref[bi, h] = st * jnp.exp(blast) + _dot_tn(vh, k_dec)
        return carry

    lax.fori_loop(0, M // C, gla_chunk, 0)

    gla_g = gla_g_ref[...]
    y_c_parts = []
    for h in range(GLA_HEADS):
        y_c_parts.append(_rms(o_ref[:, h * GLA_DV:(h + 1) * GLA_DV], gla_g))
    y_c = jnp.concatenate(y_c_parts, axis=-1) * _silu(proj(OFF_GC, D_CV))
    merged = merged + (_sigmoid(_dot(hb, w_gm_ref[:, 2 * D_MODEL:3 * D_MODEL]))
                       * _dot(y_c.astype(BF16), w_oc_ref[...]))

    out = _dot(merged.astype(BF16), w_out_ref[...])
    post_g = post_g_ref[...]
    for bi in range(BB):
        gate = mod_ref[bi, 2:3, :]
        y_ref[bi] = x_ref[bi] + gate * _rms(out[bi * TT:(bi + 1) * TT], post_g)

    @pl.when(t == nt - 1)
    def _():
        for bi in range(BB):
            for h in range(GLA_HEADS):
                sfin_ref[bi, h] = st_ref[bi, h].T


def _const_spec(shape):
    nd = len(shape)
    return pl.BlockSpec(shape, lambda b, t: (0,) * nd, pipeline_mode=pl.Buffered(1))


def _layer_tiles(B, T):
    if T >= 256:
        BB, TT = 1, 256
    else:
        BB, TT = min(B, 256 // T), T
    L = min(T, SGU_LEN)
    C = 64
    assert T % TT == 0 and B % BB == 0 and TT % L == 0 and TT % C == 0 and TT >= HIST_ROWS
    return BB, TT, L, C


def _vmem_limit(BB, TT, weights):
    M = BB * TT
    wbytes = sum(int(w.size) * w.dtype.itemsize for w in weights)
    io = 2 * 2 * M * D_MODEL * 4
    state = 2 * 2 * BB * (GLA_HEADS * GLA_DK * GLA_DV + HIST_ROWS * D_A) * 4
    scratch = M * (D_MODEL * 2 + 3 * D_CK * 4 + D_CV * 2 + D_CV * 4) + BB * (TT + HIST_ROWS) * D_A * 4 \
        + BB * GLA_HEADS * GLA_DK * GLA_DV * 4
    temps = M * (3 * D_MODEL + 6 * D_MODEL) * 4
    return min(wbytes + io + state + scratch + temps, V7X_VMEM_BYTES - 4 * 1024 * 1024)


def _mixer_layer(x, mod, hist0, s0, lw, pos0, emit_vn):
    B, T, D = x.shape
    BB, TT, L, C = _layer_tiles(B, T)
    M = BB * TT
    sgu_w = lw["sgu_w"][:, :L, :L]
    sgu_b = jnp.repeat(lw["sgu_b"][:, :L].T, SGU_GW, axis=1)
    weights = [lw["pre_g"], lw["post_g"], lw["w_main"], lw["w_lr"], lw["w_gm"], lw["pool_w"],
               lw["pool_scale"], lw["sgu_g"], sgu_w, sgu_b, lw["wa2"], lw["ba"], lw["gla_g"],
               lw["w_oa"], lw["w_ob"], lw["w_oc"], lw["w_out"]]

    grid = (B // BB, T // TT)
    in_specs = [
        pl.BlockSpec((BB, TT, D), lambda b, t: (b, t, 0)),
        pl.BlockSpec((BB, 3, D), lambda b, t: (b, 0, 0)),
        pl.BlockSpec((BB, HIST_ROWS, D_A), lambda b, t: (b, 0, 0)),
        pl.BlockSpec((BB, GLA_HEADS, GLA_DK, GLA_DV), lambda b, t: (b, 0, 0, 0)),
    ] + [_const_spec(w.shape) for w in weights]
    out_shape = [
        jax.ShapeDtypeStruct((B, T, D), F32),
        jax.ShapeDtypeStruct((B, HIST_ROWS, D_A), F32),
        jax.ShapeDtypeStruct((B, GLA_HEADS, GLA_DK, GLA_DV), F32),
    ]
    out_specs = [
        pl.BlockSpec((BB, TT, D), lambda b, t: (b, t, 0)),
        pl.BlockSpec((BB, HIST_ROWS, D_A), lambda b, t: (b, 0, 0)),
        pl.BlockSpec((BB, GLA_HEADS, GLA_DK, GLA_DV), lambda b, t: (b, 0, 0, 0)),
    ]
    if emit_vn:
        out_shape.append(jax.ShapeDtypeStruct((B, T, D_B), F32))
        out_specs.append(pl.BlockSpec((BB, TT, D_B), lambda b, t: (b, t, 0)))
    scratch_shapes = [
        pltpu.VMEM((M, D), BF16),
        pltpu.VMEM((BB, HIST_ROWS + TT, D_A), F32),
        pltpu.VMEM((M, D_CK), F32),
        pltpu.VMEM((M, D_CK), F32),
        pltpu.VMEM((M, D_CV), BF16),
        pltpu.VMEM((M, D_CK), F32),
        pltpu.VMEM((M, D_CV), F32),
        pltpu.VMEM((BB, GLA_HEADS, GLA_DV, GLA_DK), F32),
    ]
    body = functools.partial(_layer_kernel, BB=BB, TT=TT, L=L, C=C, pos0=pos0, emit_vn=emit_vn)
    return pl.pallas_call(
        body,
        grid=grid,
        in_specs=in_specs,
        out_specs=out_specs,
        out_shape=out_shape,
        scratch_shapes=scratch_shapes,
        compiler_params=pltpu.CompilerParams(
            dimension_semantics=("arbitrary", "arbitrary"),
            vmem_limit_bytes=_vmem_limit(BB, TT, weights)),
        name="mixer_layer_T%d" % T,
    )(x, mod, hist0, s0, *weights)


def _pack_layer_weights(l, pre_norm_g, post_norm_g, w_in, pool_w, pool_scale, sgu_norm_g, sgu_w, sgu_b,
                        gla_wa2, gla_ba, gla_norm_g, w_oa, w_ob, w_oc, w_out):
    w = w_in[l]
    lr0 = D_MAIN
    gm0 = D_MAIN + GLA_RANK
    return {
        "pre_g": pre_norm_g[l][None, :],
        "post_g": post_norm_g[l][None, :],
        "w_main": w[:, :D_MAIN].astype(BF16),
        "w_lr": jnp.pad(w[:, lr0:gm0], ((0, 0), (0, LANE - GLA_RANK))).astype(BF16),
        "w_gm": w[:, gm0:].astype(BF16),
        "pool_w": pool_w[l].astype(BF16),
        "pool_scale": pool_scale[l][None, :],
        "sgu_g": sgu_norm_g[l][None, :],
        "sgu_w": sgu_w[l],
        "sgu_b": sgu_b[l],
        "wa2": jnp.pad(gla_wa2[l], ((0, LANE - GLA_RANK), (0, 0))).astype(BF16),
        "ba": gla_ba[l][None, :],
        "gla_g": gla_norm_g[l][None, :],
        "w_oa": w_oa[l].astype(BF16),
        "w_ob": w_ob[l].astype(BF16),
        "w_oc": w_oc[l].astype(BF16),
        "w_out": w_out[l].astype(BF16),
    }


def kernel(x_prompt, x_sample, state_pool, state_gla, c_prompt, c_sample, ada_w, ada_b, pre_norm_g,
           post_norm_g, w_in, pool_w, pool_scale, sgu_norm_g, sgu_w, sgu_b, gla_wa2, gla_ba, gla_norm_g,
           w_oa, w_ob, w_oc, w_out):
    bp = x_prompt.shape[0]
    bs = x_sample.shape[0]
    mod = _ada_modulation(jnp.concatenate([c_prompt, c_sample], axis=0), ada_w, ada_b)
    mod = mod.reshape(DEPTH, bp + bs, 3, D_MODEL)

    hist_zero = jnp.zeros((bp, HIST_ROWS, D_A), F32)
    s_zero = jnp.zeros((bp, GLA_HEADS, GLA_DK, GLA_DV), F32)
    hist_sample = jnp.pad(state_pool, ((0, 0), (0, 0), (HIST_ROWS - POOL_HIST, 0), (0, 0)))

    xp, xs = x_prompt, x_sample
    pool_p, gla_p, pool_s, gla_s, sgu_s = [], [], [], [], []
    for l in range(DEPTH):
        lw = _pack_layer_weights(l, pre_norm_g, post_norm_g, w_in, pool_w, pool_scale, sgu_norm_g, sgu_w,
                                 sgu_b, gla_wa2, gla_ba, gla_norm_g, w_oa, w_ob, w_oc, w_out)
        xp, hp, sp = _mixer_layer(xp, mod[l, :bp], hist_zero, s_zero, lw, 0, False)
        xs, hs, ss, vs = _mixer_layer(xs, mod[l, bp:], hist_sample[l], state_gla[l], lw, PAST_LEN, True)
        pool_p.append(hp[:, HIST_ROWS - POOL_HIST:])
        gla_p.append(sp)
        pool_s.append(hs[:, HIST_ROWS - POOL_HIST:])
        gla_s.append(ss)
        sgu_s.append(vs)
    return (xp, xs, jnp.stack(pool_p), jnp.stack(gla_p), jnp.stack(pool_s), jnp.stack(gla_s),
            jnp.stack(sgu_s))
```

```python
import functools

import jax
import jax.numpy as jnp
from jax import lax
from jax.experimental import pallas as pl
from jax.experimental.pallas import tpu as pltpu

D_MODEL = 1024
DEPTH = 2
PAST_LEN = 2048
EPS = 1e-6
POOL_WINDOWS = (2, 4, 8, 16)
POOL_GW = 128
D_A = 512
POOL_HIST = 15
HIST_ROWS = 16
SGU_LEN = 128
SGU_GROUPS = 4
D_B = 512
SGU_GW = 128
GLA_HEADS = 4
GLA_DK = 128
GLA_DV = 256
D_CK = 512
D_CV = 1024
GLA_RANK = 16
GLA_NORMALIZER = 16.0
GLA_SUB = 16
LANE = 128
V7X_VMEM_BYTES = 64 * 1024 * 1024

OFF_A, OFF_GA, OFF_U, OFF_VB, OFF_GB = 0, 512, 1024, 1536, 2048
OFF_Q, OFF_K, OFF_VC, OFF_GC = 2560, 3072, 3584, 4608
D_MAIN = 5632

F32 = jnp.float32
BF16 = jnp.bfloat16


def _dot(a, b):
    return jnp.dot(a, b, preferred_element_type=F32)


def _dot_nt(a, b):
    return lax.dot_general(a, b, (((1,), (1,)), ((), ())), preferred_element_type=F32)


def _dot_tn(a, b):
    return lax.dot_general(a, b, (((0,), (0,)), ((), ())), preferred_element_type=F32)


def _sigmoid(x):
    return 1.0 / (1.0 + jnp.exp(-x))


def _silu(x):
    return x * _sigmoid(x)


def _rms(x, g):
    return x * lax.rsqrt(jnp.mean(x * x, axis=-1, keepdims=True) + EPS) * g


def _ada_kernel(c_ref, w_ref, b_ref, o_ref):
    c = c_ref[...]
    o_ref[0] = _dot(_silu(c), w_ref[0]) + b_ref[0]


def _ada_modulation(c_all, ada_w, ada_b):
    n = c_all.shape[0]
    nt = 3
    return pl.pallas_call(
        _ada_kernel,
        grid=(DEPTH, nt),
        in_specs=[
            pl.BlockSpec((n, D_MODEL), lambda l, j: (0, 0)),
            pl.BlockSpec((1, D_MODEL, D_MODEL), lambda l, j: (l, 0, j)),
            pl.BlockSpec((1, 1, D_MODEL), lambda l, j: (l, 0, j)),
        ],
        out_specs=pl.BlockSpec((1, n, D_MODEL), lambda l, j: (l, 0, j)),
        out_shape=jax.ShapeDtypeStruct((DEPTH, n, 3 * D_MODEL), F32),
        name="ada_modulation",
    )(c_all, ada_w, ada_b.reshape(DEPTH, 1, 3 * D_MODEL))


def _layer_kernel(x_ref, mod_ref, hist0_ref, s0_ref,
                  pre_g_ref, post_g_ref, w_main_ref, w_lr_ref, w_gm_ref,
                  pool_w_ref, pool_scale_ref, sgu_g_ref, sgu_w_ref, sgu_b_ref,
                  wa2_ref, ba_ref, gla_g_ref, w_oa_ref, w_ob_ref, w_oc_ref, w_out_ref,
                  *rest, BB, TT, L, C, pos0, emit_vn):
    if emit_vn:
        y_ref, hist_ref, sfin_ref, vn_ref = rest[:4]
        scratch = rest[4:]
    else:
        y_ref, hist_ref, sfin_ref = rest[:3]
        vn_ref = None
        scratch = rest[3:]
    hb_ref, ext_ref, q_ref, k_ref, v_ref, la_ref, o_ref, st_ref = scratch

    M = BB * TT
    t = pl.program_id(1)
    nt = pl.num_programs(1)

    @pl.when(t == 0)
    def _():
        ext_ref[:, 0:HIST_ROWS, :] = hist0_ref[...]
        for bi in range(BB):
            for h in range(GLA_HEADS):
                st_ref[bi, h] = s0_ref[bi, h].T

    pre_g = pre_g_ref[...]
    for bi in range(BB):
        xb = x_ref[bi]
        shift = mod_ref[bi, 0:1, :]
        scale = mod_ref[bi, 1:2, :]
        hmod = _rms(xb, pre_g) * (1.0 + scale) + shift
        hb_ref[bi * TT:(bi + 1) * TT, :] = hmod.astype(BF16)
    hb = hb_ref[...]

    def proj(off, width):
        return _dot(hb, w_main_ref[:, off:off + width])

    a = proj(OFF_A, D_A)
    row = lax.broadcasted_iota(jnp.int32, (TT, POOL_GW), 0)
    pos1 = row + (t * TT + pos0 + 1)
    y_a_parts = []
    for bi in range(BB):
        ab = a[bi * TT:(bi + 1) * TT]
        ext_ref[bi, HIST_ROWS:HIST_ROWS + TT, :] = ab
        groups = []
        for gi, w in enumerate(POOL_WINDOWS):
            ls = slice(gi * POOL_GW, (gi + 1) * POOL_GW)
            terms = [ext_ref[bi, HIST_ROWS - k:HIST_ROWS - k + TT, ls] for k in range(w)]
            while len(terms) > 1:
                terms = [terms[i] + terms[i + 1] for i in range(0, len(terms), 2)]
            cnt = jnp.minimum(pos1, w).astype(F32)
            dgi = terms[0] / cnt - ab[:, ls]
            groups.append(_dot(dgi.astype(BF16), pool_w_ref[gi]))
        y_a_parts.append(jnp.concatenate(groups, axis=-1))
        tail = ext_ref[bi, TT:TT + HIST_ROWS, :]
        ext_ref[bi, 0:HIST_ROWS, :] = tail
        hist_ref[bi] = tail
    y_a = y_a_parts[0] if BB == 1 else jnp.concatenate(y_a_parts, axis=0)
    g_a = proj(OFF_GA, D_A)
    y_a = y_a * pool_scale_ref[...] * _silu(g_a)
    merged = _sigmoid(_dot(hb, w_gm_ref[:, 0:D_MODEL])) * _dot(y_a.astype(BF16), w_oa_ref[...])

    vb = proj(OFF_VB, D_B)
    mu = jnp.mean(vb, axis=-1, keepdims=True)
    vc = vb - mu
    vn = vc * lax.rsqrt(jnp.mean(vc * vc, axis=-1, keepdims=True) + EPS) * sgu_g_ref[...]
    if emit_vn:
        for bi in range(BB):
            vn_ref[bi] = vn[bi * TT:(bi + 1) * TT]
    vnb = vn.astype(BF16)
    ri = lax.broadcasted_iota(jnp.int32, (L, L), 0)
    ci_ = lax.broadcasted_iota(jnp.int32, (L, L), 1)
    causal = ri >= ci_
    wms = [jnp.where(causal, sgu_w_ref[g], 0.0).astype(BF16) for g in range(SGU_GROUPS)]
    sgu_bias = sgu_b_ref[...]
    s_rows = []
    for c0 in range(0, M, L):
        parts = [_dot(wms[g], vnb[c0:c0 + L, g * SGU_GW:(g + 1) * SGU_GW])
                 for g in range(SGU_GROUPS)]
        s_rows.append(jnp.concatenate(parts, axis=-1) + sgu_bias)
    s_all = s_rows[0] if len(s_rows) == 1 else jnp.concatenate(s_rows, axis=0)
    y_b = proj(OFF_U, D_B) * s_all * _silu(proj(OFF_GB, D_B))
    merged = merged + (_sigmoid(_dot(hb, w_gm_ref[:, D_MODEL:2 * D_MODEL]))
                       * _dot(y_b.astype(BF16), w_ob_ref[...]))

    q_ref[...] = proj(OFF_Q, D_CK) * (GLA_DK ** -0.5)
    k_ref[...] = proj(OFF_K, D_CK)
    v_ref[...] = proj(OFF_VC, D_CV).astype(BF16)
    z_lr = _dot(hb, w_lr_ref[...])
    pre = _dot(z_lr.astype(BF16), wa2_ref[...]) + ba_ref[...]
    log_sig = jnp.minimum(pre, 0.0) - jnp.log1p(jnp.exp(-jnp.abs(pre)))
    la_ref[...] = log_sig / GLA_NORMALIZER

    cr = lax.broadcasted_iota(jnp.int32, (C, C), 0)
    cc = lax.broadcasted_iota(jnp.int32, (C, C), 1)
    tri = (cr >= cc).astype(BF16)
    sub_shift = GLA_SUB.bit_length() - 1
    mask_diag = (cr >= cc) & ((cr >> sub_shift) == (cc >> sub_shift))
    rowi = lax.broadcasted_iota(jnp.int32, (C, GLA_DK), 0)
    levels = []
    s = GLA_SUB
    while 2 * s <= C:
        sh = s.bit_length() - 1
        levels.append((s, ((rowi >> sh) & 1) == 1, (cr >> (sh + 1)) == (cc >> (sh + 1))))
        s *= 2
    chunks_per_seq = TT // C

    def rows_bcast(src, idx, n):
        return jnp.broadcast_to(src[idx:idx + 1, :], (n, src.shape[1]))

    n_chunks = M // C
    head_chunks = [(ci, h) for ci in range(n_chunks) for h in range(GLA_HEADS)]

    b_chunks = []
    for ci in range(n_chunks):
        la = la_ref[ci * C:(ci + 1) * C, :]
        p0 = la.astype(BF16)
        r1 = la - p0.astype(F32)
        p1 = r1.astype(BF16)
        p2 = (r1 - p1.astype(F32)).astype(BF16)
        b_chunks.append(_dot(tri, p0) + _dot(tri, p1) + _dot(tri, p2))

    prepped = []
    for ci, h in head_chunks:
        ks = slice(h * GLA_DK, (h + 1) * GLA_DK)
        b = b_chunks[ci][:, ks]
        qh = q_ref[ci * C:(ci + 1) * C, ks]
        kh = k_ref[ci * C:(ci + 1) * C, ks]
        blast = b[C - 1:C, :]
        q_in = (qh * jnp.exp(b)).astype(BF16)
        k_dec = (kh * jnp.exp(blast - b)).astype(BF16)
        ref_sub = jnp.concatenate(
            [jnp.zeros((GLA_SUB, GLA_DK), F32)]
            + [rows_bcast(b, GLA_SUB * i - 1, GLA_SUB) for i in range(1, C // GLA_SUB)], axis=0)
        qd = qh * jnp.exp(b - ref_sub)
        kd = (kh * jnp.exp(ref_sub - b)).astype(BF16)
        pairs = [(qd.astype(BF16), kd)]
        for (sz, upper, _) in levels:
            ref_mid = jnp.concatenate(
                [rows_bcast(b, p + sz - 1, 2 * sz) for p in range(0, C, 2 * sz)], axis=0)
            if sz == GLA_SUB:
                ql = jnp.where(upper, qd, 0.0)
            else:
                ql = jnp.where(upper, qh * jnp.exp(jnp.minimum(b - ref_mid, 0.0)), 0.0)
            kl = jnp.where(upper, 0.0, kh * jnp.exp(jnp.minimum(ref_mid - b, 0.0)))
            pairs.append((ql.astype(BF16), kl.astype(BF16)))
        prepped.append((q_in, k_dec, jnp.exp(blast), pairs))

    atts = []
    for (_, _, _, pairs) in prepped:
        att = jnp.where(mask_diag, _dot_nt(*pairs[0]), 0.0)
        for (sz, _, same_blk), pr in zip(levels, pairs[1:]):
            part = _dot_nt(*pr)
            att = att + (part if 2 * sz == C else jnp.where(same_blk, part, 0.0))
        atts.append(att.astype(BF16))

    for (ci, h), (q_in, k_dec, dec, _), att in zip(head_chunks, prepped, atts):
        bi = ci // chunks_per_seq
        vs = slice(h * GLA_DV, (h + 1) * GLA_DV)
        vh = v_ref[ci * C:(ci + 1) * C, vs]
        st = st_ref[bi, h]
        o_ref[ci * C:(ci + 1) * C, vs] = _dot(att, vh) + _dot_nt(q_in, st.astype(BF16))
        st_ref[bi, h] = st * dec + _dot_tn(vh, k_dec)

    gla_g = gla_g_ref[...]
    y_c_parts = []
    for h in range(GLA_HEADS):
        y_c_parts.append(_rms(o_ref[:, h * GLA_DV:(h + 1) * GLA_DV], gla_g))
    y_c = jnp.concatenate(y_c_parts, axis=-1) * _silu(proj(OFF_GC, D_CV))
    merged = merged + (_sigmoid(_dot(hb, w_gm_ref[:, 2 * D_MODEL:3 * D_MODEL]))
                       * _dot(y_c.astype(BF16), w_oc_ref[...]))

    out = _dot(merged.astype(BF16), w_out_ref[...])
    post_g = post_g_ref[...]
    for bi in range(BB):
        gate = mod_ref[bi, 2:3, :]
        y_ref[bi] = x_ref[bi] + gate * _rms(out[bi * TT:(bi + 1) * TT], post_g)

    @pl.when(t == nt - 1)
    def _():
        for bi in range(BB):
            for h in range(GLA_HEADS):
                sfin_ref[bi, h] = st_ref[bi, h].T


def _const_spec(shape):
    nd = len(shape)
    return pl.BlockSpec(shape, lambda b, t: (0,) * nd, pipeline_mode=pl.Buffered(1))


def _layer_tiles(B, T):
    if T >= 256:
        BB, TT = 1, 512
    else:
        BB, TT = min(B, 256 // T), T
    L = min(T, SGU_LEN)
    C = 64
    assert T % TT == 0 and B % BB == 0 and TT % L == 0 and TT % C == 0 and TT >= HIST_ROWS
    return BB, TT, L, C


def _vmem_limit(BB, TT, weights):
    M = BB * TT
    wbytes = sum(int(w.size) * w.dtype.itemsize for w in weights)
    io = 2 * 2 * M * D_MODEL * 4
    state = 2 * 2 * BB * (GLA_HEADS * GLA_DK * GLA_DV + HIST_ROWS * D_A) * 4
    scratch = M * (D_MODEL * 2 + 3 * D_CK * 4 + D_CV * 2 + D_CV * 4) + BB * (TT + HIST_ROWS) * D_A * 4 \
        + BB * GLA_HEADS * GLA_DK * GLA_DV * 4
    temps = M * (3 * D_MODEL + 6 * D_MODEL) * 4
    return min(wbytes + io + state + scratch + temps, V7X_VMEM_BYTES - 4 * 1024 * 1024)


def _mixer_layer(x, mod, hist0, s0, lw, pos0, emit_vn):
    B, T, D = x.shape
    BB, TT, L, C = _layer_tiles(B, T)
    M = BB * TT
    sgu_w = lw["sgu_w"][:, :L, :L]
    sgu_b = jnp.repeat(lw["sgu_b"][:, :L].T, SGU_GW, axis=1)
    weights = [lw["pre_g"], lw["post_g"], lw["w_main"], lw["w_lr"], lw["w_gm"], lw["pool_w"],
               lw["pool_scale"], lw["sgu_g"], sgu_w, sgu_b, lw["wa2"], lw["ba"], lw["gla_g"],
               lw["w_oa"], lw["w_ob"], lw["w_oc"], lw["w_out"]]

    grid = (B // BB, T // TT)
    in_specs = [
        pl.BlockSpec((BB, TT, D), lambda b, t: (b, t, 0)),
        pl.BlockSpec((BB, 3, D), lambda b, t: (b, 0, 0)),
        pl.BlockSpec((BB, HIST_ROWS, D_A), lambda b, t: (b, 0, 0)),
        pl.BlockSpec((BB, GLA_HEADS, GLA_DK, GLA_DV), lambda b, t: (b, 0, 0, 0)),
    ] + [_const_spec(w.shape) for w in weights]
    out_shape = [
        jax.ShapeDtypeStruct((B, T, D), F32),
        jax.ShapeDtypeStruct((B, HIST_ROWS, D_A), F32),
        jax.ShapeDtypeStruct((B, GLA_HEADS, GLA_DK, GLA_DV), F32),
    ]
    out_specs = [
        pl.BlockSpec((BB, TT, D), lambda b, t: (b, t, 0)),
        pl.BlockSpec((BB, HIST_ROWS, D_A), lambda b, t: (b, 0, 0)),
        pl.BlockSpec((BB, GLA_HEADS, GLA_DK, GLA_DV), lambda b, t: (b, 0, 0, 0)),
    ]
    if emit_vn:
        out_shape.append(jax.ShapeDtypeStruct((B, T, D_B), F32))
        out_specs.append(pl.BlockSpec((BB, TT, D_B), lambda b, t: (b, t, 0)))
    scratch_shapes = [
        pltpu.VMEM((M, D), BF16),
        pltpu.VMEM((BB, HIST_ROWS + TT, D_A), F32),
        pltpu.VMEM((M, D_CK), F32),
        pltpu.VMEM((M, D_CK), F32),
        pltpu.VMEM((M, D_CV), BF16),
        pltpu.VMEM((M, D_CK), F32),
        pltpu.VMEM((M, D_CV), F32),
        pltpu.VMEM((BB, GLA_HEADS, GLA_DV, GLA_DK), F32),
    ]
    body = functools.partial(_layer_kernel, BB=BB, TT=TT, L=L, C=C, pos0=pos0, emit_vn=emit_vn)
    return pl.pallas_call(
        body,
        grid=grid,
        in_specs=in_specs,
        out_specs=out_specs,
        out_shape=out_shape,
        scratch_shapes=scratch_shapes,
        compiler_params=pltpu.CompilerParams(
            dimension_semantics=("arbitrary", "arbitrary"),
            vmem_limit_bytes=_vmem_limit(BB, TT, weights)),
        name="mixer_layer_T%d" % T,
    )(x, mod, hist0, s0, *weights)


def _pack_layer_weights(l, pre_norm_g, post_norm_g, w_in, pool_w, pool_scale, sgu_norm_g, sgu_w, sgu_b,
                        gla_wa2, gla_ba, gla_norm_g, w_oa, w_ob, w_oc, w_out):
    w = w_in[l]
    lr0 = D_MAIN
    gm0 = D_MAIN + GLA_RANK
    return {
        "pre_g": pre_norm_g[l][None, :],
        "post_g": post_norm_g[l][None, :],
        "w_main": w[:, :D_MAIN].astype(BF16),
        "w_lr": jnp.pad(w[:, lr0:gm0], ((0, 0), (0, LANE - GLA_RANK))).astype(BF16),
        "w_gm": w[:, gm0:].astype(BF16),
        "pool_w": pool_w[l].astype(BF16),
        "pool_scale": pool_scale[l][None, :],
        "sgu_g": sgu_norm_g[l][None, :],
        "sgu_w": sgu_w[l],
        "sgu_b": sgu_b[l],
        "wa2": jnp.pad(gla_wa2[l], ((0, LANE - GLA_RANK), (0, 0))).astype(BF16),
        "ba": gla_ba[l][None, :],
        "gla_g": gla_norm_g[l][None, :],
        "w_oa": w_oa[l].astype(BF16),
        "w_ob": w_ob[l].astype(BF16),
        "w_oc": w_oc[l].astype(BF16),
        "w_out": w_out[l].astype(BF16),
    }


def kernel(x_prompt, x_sample, state_pool, state_gla, c_prompt, c_sample, ada_w, ada_b, pre_norm_g,
           post_norm_g, w_in, pool_w, pool_scale, sgu_norm_g, sgu_w, sgu_b, gla_wa2, gla_ba, gla_norm_g,
           w_oa, w_ob, w_oc, w_out):
    bp = x_prompt.shape[0]
    bs = x_sample.shape[0]
    mod = _ada_modulation(jnp.concatenate([c_prompt, c_sample], axis=0), ada_w, ada_b)
    mod = mod.reshape(DEPTH, bp + bs, 3, D_MODEL)

    hist_zero = jnp.zeros((bp, HIST_ROWS, D_A), F32)
    s_zero = jnp.zeros((bp, GLA_HEADS, GLA_DK, GLA_DV), F32)
    hist_sample = jnp.pad(state_pool, ((0, 0), (0, 0), (HIST_ROWS - POOL_HIST, 0), (0, 0)))

    xp, xs = x_prompt, x_sample
    pool_p, gla_p, pool_s, gla_s, sgu_s = [], [], [], [], []
    for l in range(DEPTH):
        lw = _pack_layer_weights(l, pre_norm_g, post_norm_g, w_in, pool_w, pool_scale, sgu_norm_g, sgu_w,
                                 sgu_b, gla_wa2, gla_ba, gla_norm_g, w_oa, w_ob, w_oc, w_out)
        xp, hp, sp = _mixer_layer(xp, mod[l, :bp], hist_zero, s_zero, lw, 0, False)
        xs, hs, ss, vs = _mixer_layer(xs, mod[l, bp:], hist_sample[l], state_gla[l], lw, PAST_LEN, True)
        pool_p.append(hp[:, HIST_ROWS - POOL_HIST:])
        gla_p.append(sp)
        pool_s.append(hs[:, HIST_ROWS - POOL_HIST:])
        gla_s.append(ss)
        sgu_s.append(vs)
    return (xp, xs, jnp.stack(pool_p), jnp.stack(gla_p), jnp.stack(pool_s), jnp.stack(gla_s),
            jnp.stack(sgu_s))
```

```python
import functools
import math

import jax
import jax.numpy as jnp
from jax import lax
from jax.experimental import pallas as pl
from jax.experimental.pallas import tpu as pltpu

D_MODEL = 1024
DEPTH = 2
PAST_LEN = 2048
EPS = 1e-6
POOL_WINDOWS = (2, 4, 8, 16)
POOL_GW = 128
D_A = 512
POOL_HIST = 15
HIST_ROWS = 16
SGU_LEN = 128
SGU_GROUPS = 4
D_B = 512
SGU_GW = 128
GLA_HEADS = 4
GLA_DK = 128
GLA_DV = 256
D_CK = 512
D_CV = 1024
GLA_RANK = 16
GLA_NORMALIZER = 16.0
GLA_SUB = 16
LANE = 128
V7X_VMEM_BYTES = 64 * 1024 * 1024
LOG2E = math.log2(math.e)

OFF_A, OFF_GA, OFF_U, OFF_VB, OFF_GB = 0, 512, 1024, 1536, 2048
OFF_Q, OFF_K, OFF_VC, OFF_GC = 2560, 3072, 3584, 4608
OFF_LR = 5632
OFF_GM = OFF_LR + LANE
D_PACK = OFF_GM + 3 * D_MODEL
ROW_OA, ROW_OB, ROW_OC, ROW_OUT = 0, D_A, D_A + D_B, D_A + D_B + D_CV
D_OSTACK = ROW_OUT + D_MODEL

F32 = jnp.float32
BF16 = jnp.bfloat16


def _dot(a, b):
    return jnp.dot(a, b, preferred_element_type=F32)


def _dot_nt(a, b):
    return lax.dot_general(a, b, (((1,), (1,)), ((), ())), preferred_element_type=F32)


def _dot_tn(a, b):
    return lax.dot_general(a, b, (((0,), (0,)), ((), ())), preferred_element_type=F32)


def _sigmoid(x):
    return 1.0 / (1.0 + jnp.exp(-x))


def _silu(x):
    return x * _sigmoid(x)


def _rms(x, g):
    return x * lax.rsqrt(jnp.mean(x * x, axis=-1, keepdims=True) + EPS) * g


def _ada_kernel(c_ref, w_ref, b_ref, o_ref):
    c = c_ref[...]
    o_ref[0] = _dot(_silu(c), w_ref[0]) + b_ref[0]


def _ada_modulation(c_all, ada_w, ada_b):
    n = c_all.shape[0]
    nt = 3
    return pl.pallas_call(
        _ada_kernel,
        grid=(DEPTH, nt),
        in_specs=[
            pl.BlockSpec((n, D_MODEL), lambda l, j: (0, 0)),
            pl.BlockSpec((1, D_MODEL, D_MODEL), lambda l, j: (l, 0, j)),
            pl.BlockSpec((1, 1, D_MODEL), lambda l, j: (l, 0, j)),
        ],
        out_specs=pl.BlockSpec((1, n, D_MODEL), lambda l, j: (l, 0, j)),
        out_shape=jax.ShapeDtypeStruct((DEPTH, n, 3 * D_MODEL), F32),
        name="ada_modulation",
    )(c_all, ada_w, ada_b.reshape(DEPTH, 1, 3 * D_MODEL))


def _layer_kernel(*refs, BB, TT, L, C, pos0, has_state, emit_vn):
    refs = list(refs)
    x_ref, mod_ref = refs[:2]
    del refs[:2]
    if has_state:
        hist0_ref, s0_ref = refs[:2]
        del refs[:2]
    (pre_g_ref, post_g_ref, w_in_ref, pool_w_ref, pool_scale_ref, sgu_g_ref, sgu_w_ref, sgu_b_ref,
     wa2_ref, ba_ref, gla_g_ref, w_o_ref) = refs[:12]
    del refs[:12]
    y_ref, hist_ref, sfin_ref = refs[:3]
    del refs[:3]
    if emit_vn:
        vn_ref = refs.pop(0)
    hb_ref, ext_ref, q_ref, k_ref, v_ref, la_ref, o_ref, st_ref = refs

    M = BB * TT
    t = pl.program_id(1)
    nt = pl.num_programs(1)

    @pl.when(t == 0)
    def _():
        if has_state:
            ext_ref[:, 0:HIST_ROWS, :] = hist0_ref[...]
            for bi in range(BB):
                for h in range(GLA_HEADS):
                    st_ref[bi, h] = s0_ref[bi, h].T
        else:
            ext_ref[:, 0:HIST_ROWS, :] = jnp.zeros((BB, HIST_ROWS, D_A), F32)
            st_ref[...] = jnp.zeros(st_ref.shape, F32)

    pre_g = pre_g_ref[0]
    for bi in range(BB):
        xb = x_ref[bi]
        shift = mod_ref[0, bi, 0:1, :]
        scale = mod_ref[0, bi, 1:2, :]
        hmod = _rms(xb, pre_g) * (1.0 + scale) + shift
        hb_ref[bi * TT:(bi + 1) * TT, :] = hmod.astype(BF16)
    hb = hb_ref[...]

    def proj(off, width):
        return _dot(hb, w_in_ref[0, :, off:off + width])

    a = proj(OFF_A, D_A)
    row = lax.broadcasted_iota(jnp.int32, (TT, POOL_GW), 0)
    pos1 = row + (t * TT + pos0 + 1)
    y_a_parts = []
    for bi in range(BB):
        ab = a[bi * TT:(bi + 1) * TT]
        ext_ref[bi, HIST_ROWS:HIST_ROWS + TT, :] = ab
        groups = []
        for gi, w in enumerate(POOL_WINDOWS):
            ls = slice(gi * POOL_GW, (gi + 1) * POOL_GW)
            terms = [ext_ref[bi, HIST_ROWS - k:HIST_ROWS - k + TT, ls] for k in range(w)]
            while len(terms) > 1:
                terms = [terms[i] + terms[i + 1] for i in range(0, len(terms), 2)]
            cnt = jnp.minimum(pos1, w).astype(F32)
            dgi = terms[0] / cnt - ab[:, ls]
            groups.append(_dot(dgi.astype(BF16), pool_w_ref[0, gi].astype(BF16)))
        y_a_parts.append(jnp.concatenate(groups, axis=-1))
        tail = ext_ref[bi, TT:TT + HIST_ROWS, :]
        ext_ref[bi, 0:HIST_ROWS, :] = tail
        hist_ref[bi] = tail
    y_a = y_a_parts[0] if BB == 1 else jnp.concatenate(y_a_parts, axis=0)
    g_a = proj(OFF_GA, D_A)
    y_a = y_a * pool_scale_ref[0] * _silu(g_a)
    merged = (_sigmoid(proj(OFF_GM, D_MODEL))
              * _dot(y_a.astype(BF16), w_o_ref[0, ROW_OA:ROW_OA + D_A, :]))

    vb = proj(OFF_VB, D_B)
    mu = jnp.mean(vb, axis=-1, keepdims=True)
    vc = vb - mu
    vn = vc * lax.rsqrt(jnp.mean(vc * vc, axis=-1, keepdims=True) + EPS) * sgu_g_ref[0]
    if emit_vn:
        for bi in range(BB):
            vn_ref[bi] = vn[bi * TT:(bi + 1) * TT]
    vnb = vn.astype(BF16)
    ri = lax.broadcasted_iota(jnp.int32, (L, L), 0)
    ci_ = lax.broadcasted_iota(jnp.int32, (L, L), 1)
    causal = ri >= ci_
    wms = [jnp.where(causal, sgu_w_ref[0, g, 0:L, 0:L], 0.0).astype(BF16) for g in range(SGU_GROUPS)]
    sgu_bias = sgu_b_ref[0, 0:L, :]
    s_rows = []
    for c0 in range(0, M, L):
        parts = [_dot(wms[g], vnb[c0:c0 + L, g * SGU_GW:(g + 1) * SGU_GW])
                 for g in range(SGU_GROUPS)]
        s_rows.append(jnp.concatenate(parts, axis=-1) + sgu_bias)
    s_all = s_rows[0] if len(s_rows) == 1 else jnp.concatenate(s_rows, axis=0)
    y_b = proj(OFF_U, D_B) * s_all * _silu(proj(OFF_GB, D_B))
    merged = merged + (_sigmoid(proj(OFF_GM + D_MODEL, D_MODEL))
                       * _dot(y_b.astype(BF16), w_o_ref[0, ROW_OB:ROW_OB + D_B, :]))

    q_ref[...] = proj(OFF_Q, D_CK) * (GLA_DK ** -0.5)
    k_ref[...] = proj(OFF_K, D_CK)
    v_ref[...] = proj(OFF_VC, D_CV).astype(BF16)
    z_lr = proj(OFF_LR, LANE)
    wa2 = jnp.concatenate([wa2_ref[0], jnp.zeros((LANE - GLA_RANK, D_CK), F32)], axis=0)
    pre = _dot(z_lr.astype(BF16), wa2.astype(BF16)) + ba_ref[0]
    log_sig = jnp.minimum(pre, 0.0) - jnp.log1p(jnp.exp(-jnp.abs(pre)))
    la_ref[...] = log_sig * (LOG2E / GLA_NORMALIZER)

    cr = lax.broadcasted_iota(jnp.int32, (C, C), 0)
    cc = lax.broadcasted_iota(jnp.int32, (C, C), 1)
    tri = (cr >= cc).astype(BF16)
    sub_shift = GLA_SUB.bit_length() - 1
    mask_diag = (cr >= cc) & ((cr >> sub_shift) == (cc >> sub_shift))
    nb = C // GLA_SUB
    levels = []
    s = 1
    while 2 * s <= nb:
        sh = sub_shift + s.bit_length()
        levels.append((s, None if 2 * s == nb else (cr >> sh) == (cc >> sh)))
        s *= 2
    chunks_per_seq = TT // C
    n_chunks = M // C
    head_chunks = [(ci, h) for ci in range(n_chunks) for h in range(GLA_HEADS)]

    def blk(x, i):
        return x[i * GLA_SUB:(i + 1) * GLA_SUB]

    def rows16(r):
        return jnp.broadcast_to(r, (GLA_SUB, r.shape[1]))

    def stack(blocks):
        zero = jnp.zeros((GLA_SUB, GLA_DK), F32)
        return jnp.concatenate([zero if b is None else b for b in blocks], axis=0).astype(BF16)

    b_chunks = []
    for ci in range(n_chunks):
        la = la_ref[ci * C:(ci + 1) * C, :]
        p0 = la.astype(BF16)
        r1 = la - p0.astype(F32)
        p1 = r1.astype(BF16)
        p2 = (r1 - p1.astype(F32)).astype(BF16)
        b_chunks.append(_dot(tri, p0) + _dot(tri, p1) + _dot(tri, p2))

    prepped = []
    for ci, h in head_chunks:
        ks = slice(h * GLA_DK, (h + 1) * GLA_DK)
        b = b_chunks[ci][:, ks]
        qh = q_ref[ci * C:(ci + 1) * C, ks]
        kh = k_ref[ci * C:(ci + 1) * C, ks]
        ends = [b[(i + 1) * GLA_SUB - 1:(i + 1) * GLA_SUB, :] for i in range(nb)]
        starts = [None] + ends[:-1]
        ref_start = jnp.concatenate([jnp.zeros((GLA_SUB, GLA_DK), F32)]
                                    + [rows16(e) for e in ends[:-1]], axis=0)
        ref_end = jnp.concatenate([rows16(e) for e in ends], axis=0)
        rel = b - ref_start
        qd = qh * jnp.exp2(rel)
        kd = kh * jnp.exp2(-rel)
        ke = kh * jnp.exp2(ref_end - b)

        def q_from(i, j):
            return blk(qd, i) if j == i - 1 else blk(qd, i) * rows16(jnp.exp2(starts[i] - ends[j]))

        def k_to(i, j):
            return blk(ke, i) if j == i else blk(ke, i) * rows16(jnp.exp2(ends[j] - ends[i]))

        pairs = [(qd.astype(BF16), kd.astype(BF16))]
        for (sz, _) in levels:
            qblocks, kblocks = [], []
            for i in range(nb):
                mid = (i // (2 * sz)) * 2 * sz + sz
                if (i // sz) % 2 == 1:
                    qblocks.append(q_from(i, mid - 1))
                    kblocks.append(None)
                else:
                    qblocks.append(None)
                    kblocks.append(k_to(i, mid - 1))
            pairs.append((stack(qblocks), stack(kblocks)))
        q_in = stack([blk(qd, 0)] + [blk(qd, i) * rows16(jnp.exp2(starts[i])) for i in range(1, nb)])
        k_dec = stack([k_to(i, nb - 1) for i in range(nb)])
        prepped.append((q_in, k_dec, jnp.exp2(ends[nb - 1]), pairs))

    atts = []
    for (_, _, _, pairs) in prepped:
        att = jnp.where(mask_diag, _dot_nt(*pairs[0]), 0.0)
        for (_, same_blk), pr in zip(levels, pairs[1:]):
            part = _dot_nt(*pr)
            att = att + (part if same_blk is None else jnp.where(same_blk, part, 0.0))
        atts.append(att.astype(BF16))

    for (ci, h), (q_in, k_dec, dec, _), att in zip(head_chunks, prepped, atts):
        bi = ci // chunks_per_seq
        vs = slice(h * GLA_DV, (h + 1) * GLA_DV)
        vh = v_ref[ci * C:(ci + 1) * C, vs]
        st = st_ref[bi, h]
        o_ref[ci * C:(ci + 1) * C, vs] = _dot(att, vh) + _dot_nt(q_in, st.astype(BF16))
        st_ref[bi, h] = st * dec + _dot_tn(vh, k_dec)

    gla_g = gla_g_ref[0]
    y_c_parts = []
    for h in range(GLA_HEADS):
        y_c_parts.append(_rms(o_ref[:, h * GLA_DV:(h + 1) * GLA_DV], gla_g))
    y_c = jnp.concatenate(y_c_parts, axis=-1) * _silu(proj(OFF_GC, D_CV))
    merged = merged + (_sigmoid(proj(OFF_GM + 2 * D_MODEL, D_MODEL))
                       * _dot(y_c.astype(BF16), w_o_ref[0, ROW_OC:ROW_OC + D_CV, :]))

    out = _dot(merged.astype(BF16), w_o_ref[0, ROW_OUT:ROW_OUT + D_MODEL, :])
    post_g = post_g_ref[0]
    for bi in range(BB):
        gate = mod_ref[0, bi, 2:3, :]
        y_ref[bi] = x_ref[bi] + gate * _rms(out[bi * TT:(bi + 1) * TT], post_g)

    @pl.when(t == nt - 1)
    def _():
        for bi in range(BB):
            for h in range(GLA_HEADS):
                sfin_ref[bi, h] = st_ref[bi, h].T


def _layer_tiles(B, T):
    if T >= 512:
        BB, TT = 1, 512
    else:
        BB, TT = min(B, 256 // T), T
    L = min(T, SGU_LEN)
    C = 64
    assert T % TT == 0 and B % BB == 0 and TT % L == 0 and TT % C == 0 and TT >= HIST_ROWS
    return BB, TT, L, C


def _vmem_limit(BB, TT, weights, has_state):
    M = BB * TT
    wbytes = sum(math.prod(w.shape[1:]) * w.dtype.itemsize for w in weights)
    io = 2 * 2 * M * D_MODEL * 4
    state = 2 * (2 if has_state else 1) * BB * (GLA_HEADS * GLA_DK * GLA_DV + HIST_ROWS * D_A) * 4
    scratch = M * (D_MODEL * 2 + 3 * D_CK * 4 + D_CV * 2 + D_CV * 4) + BB * (TT + HIST_ROWS) * D_A * 4 \
        + BB * GLA_HEADS * GLA_DK * GLA_DV * 4
    temps = M * (3 * D_MODEL + 6 * D_MODEL) * 4
    return min(wbytes + io + state + scratch + temps, V7X_VMEM_BYTES - 4 * 1024 * 1024)


def _mixer_layer(l, x, mod, mod_row0, state, weights, pos0, emit_vn):
    B, T, D = x.shape
    BB, TT, L, C = _layer_tiles(B, T)
    M = BB * TT
    has_state = state is not None
    mod_blk0 = mod_row0 // BB

    def wspec(w):
        nd = w.ndim
        return pl.BlockSpec((1,) + w.shape[1:], lambda b, t: (l,) + (0,) * (nd - 1),
                            pipeline_mode=pl.Buffered(1))

    grid = (B // BB, T // TT)
    inputs = [x, mod]
    in_specs = [
        pl.BlockSpec((BB, TT, D), lambda b, t: (b, t, 0)),
        pl.BlockSpec((1, BB, 3, D), lambda b, t: (l, mod_blk0 + b, 0, 0)),
    ]
    if has_state:
        inputs += list(state)
        in_specs += [
            pl.BlockSpec((BB, HIST_ROWS, D_A), lambda b, t: (b, 0, 0)),
            pl.BlockSpec((BB, GLA_HEADS, GLA_DK, GLA_DV), lambda b, t: (b, 0, 0, 0)),
        ]
    inputs += list(weights)
    in_specs += [wspec(w) for w in weights]
    out_shape = [
        jax.ShapeDtypeStruct((B, T, D), F32),
        jax.ShapeDtypeStruct((B, HIST_ROWS, D_A), F32),
        jax.ShapeDtypeStruct((B, GLA_HEADS, GLA_DK, GLA_DV), F32),
    ]
    out_specs = [
        pl.BlockSpec((BB, TT, D), lambda b, t: (b, t, 0)),
        pl.BlockSpec((BB, HIST_ROWS, D_A), lambda b, t: (b, 0, 0)),
        pl.BlockSpec((BB, GLA_HEADS, GLA_DK, GLA_DV), lambda b, t: (b, 0, 0, 0)),
    ]
    if emit_vn:
        out_shape.append(jax.ShapeDtypeStruct((B, T, D_B), F32))
        out_specs.append(pl.BlockSpec((BB, TT, D_B), lambda b, t: (b, t, 0)))
    scratch_shapes = [
        pltpu.VMEM((M, D), BF16),
        pltpu.VMEM((BB, HIST_ROWS + TT, D_A), F32),
        pltpu.VMEM((M, D_CK), F32),
        pltpu.VMEM((M, D_CK), F32),
        pltpu.VMEM((M, D_CV), BF16),
        pltpu.VMEM((M, D_CK), F32),
        pltpu.VMEM((M, D_CV), F32),
        pltpu.VMEM((BB, GLA_HEADS, GLA_DV, GLA_DK), F32),
    ]
    body = functools.partial(_layer_kernel, BB=BB, TT=TT, L=L, C=C, pos0=pos0,
                             has_state=has_state, emit_vn=emit_vn)
    return pl.pallas_call(
        body,
        grid=grid,
        in_specs=in_specs,
        out_specs=out_specs,
        out_shape=out_shape,
        scratch_shapes=scratch_shapes,
        compiler_params=pltpu.CompilerParams(
            dimension_semantics=("arbitrary", "arbitrary"),
            vmem_limit_bytes=_vmem_limit(BB, TT, weights, has_state)),
        name="mixer_layer_T%d" % T,
    )(*inputs)


def kernel(x_prompt, x_sample, state_pool, state_gla, c_prompt, c_sample, ada_w, ada_b, pre_norm_g,
           post_norm_g, w_in, pool_w, pool_scale, sgu_norm_g, sgu_w, sgu_b, gla_wa2, gla_ba, gla_norm_g,
           w_oa, w_ob, w_oc, w_out):
    bp = x_prompt.shape[0]
    bs = x_sample.shape[0]
    mod = _ada_modulation(jnp.concatenate([c_prompt, c_sample], axis=0), ada_w, ada_b)
    mod = mod.reshape(DEPTH, bp + bs, 3, D_MODEL)

    lr0 = OFF_LR
    gm0 = OFF_LR + GLA_RANK
    w_pack = jnp.concatenate(
        [w_in[..., :lr0].astype(BF16),
         jnp.pad(w_in[..., lr0:gm0].astype(BF16), ((0, 0), (0, 0), (0, LANE - GLA_RANK))),
         w_in[..., gm0:].astype(BF16)], axis=-1)
    w_o = jnp.concatenate([w_oa, w_ob, w_oc, w_out], axis=1).astype(BF16)
    sgu_bias = jnp.repeat(jnp.swapaxes(sgu_b, 1, 2), SGU_GW, axis=2)
    vec = lambda v: v[:, None, :]
    weights = [vec(pre_norm_g), vec(post_norm_g), w_pack, pool_w, vec(pool_scale), vec(sgu_norm_g),
               sgu_w, sgu_bias, gla_wa2, vec(gla_ba), vec(gla_norm_g), w_o]
    hist_sample = jnp.pad(state_pool, ((0, 0), (0, 0), (HIST_ROWS - POOL_HIST, 0), (0, 0)))

    xp, xs = x_prompt, x_sample
    pool_p, gla_p, pool_s, gla_s, sgu_s = [], [], [], [], []
    for l in range(DEPTH):
        xp, hp, sp = _mixer_layer(l, xp, mod, 0, None, weights, 0, False)
        xs, hs, ss, vs = _mixer_layer(l, xs, mod, bp, (hist_sample[l], state_gla[l]), weights,
                                      PAST_LEN, True)
        pool_p.append(hp[:, HIST_ROWS - POOL_HIST:])
        gla_p.append(sp)
        pool_s.append(hs[:, HIST_ROWS - POOL_HIST:])
        gla_s.append(ss)
        sgu_s.append(vs)
    return (xp, xs, jnp.stack(pool_p), jnp.stack(gla_p), jnp.stack(pool_s), jnp.stack(gla_s),
            jnp.stack(sgu_s))
```

```python
import functools
import math

import jax
import jax.numpy as jnp
from jax import lax
from jax.experimental import pallas as pl
from jax.experimental.pallas import tpu as pltpu

D_MODEL = 1024
DEPTH = 2
PAST_LEN = 2048
EPS = 1e-6
POOL_WINDOWS = (2, 4, 8, 16)
POOL_GW = 128
D_A = 512
POOL_HIST = 15
HIST_ROWS = 16
SGU_LEN = 128
SGU_GROUPS = 4
D_B = 512
SGU_GW = 128
GLA_HEADS = 4
GLA_DK = 128
GLA_DV = 256
D_CK = 512
D_CV = 1024
GLA_RANK = 16
GLA_NORMALIZER = 16.0
GLA_SUB = 16
LANE = 128
V7X_VMEM_BYTES = 64 * 1024 * 1024
LOG2E = math.log2(math.e)

OFF_A, OFF_GA, OFF_U, OFF_VB, OFF_GB = 0, 512, 1024, 1536, 2048
OFF_Q, OFF_K, OFF_VC, OFF_GC = 2560, 3072, 3584, 4608
OFF_LR = 5632
OFF_GM = OFF_LR + LANE
D_PACK = OFF_GM + 3 * D_MODEL
ROW_OA, ROW_OB, ROW_OC, ROW_OUT = 0, D_A, D_A + D_B, D_A + D_B + D_CV
D_OSTACK = ROW_OUT + D_MODEL

F32 = jnp.float32
BF16 = jnp.bfloat16


def _dot(a, b):
    return jnp.dot(a, b, preferred_element_type=F32)


def _dot_nt(a, b):
    return lax.dot_general(a, b, (((1,), (1,)), ((), ())), preferred_element_type=F32)


def _dot_tn(a, b):
    return lax.dot_general(a, b, (((0,), (0,)), ((), ())), preferred_element_type=F32)


def _sigmoid(x):
    return 1.0 / (1.0 + jnp.exp(-x))


def _silu(x):
    return x * _sigmoid(x)


def _rms(x, g):
    return x * lax.rsqrt(jnp.mean(x * x, axis=-1, keepdims=True) + EPS) * g


def _ada_kernel(c_ref, w_ref, b_ref, o_ref):
    c = c_ref[...]
    o_ref[0] = _dot(_silu(c), w_ref[0]) + b_ref[0]


def _ada_modulation(c_all, ada_w, ada_b):
    n = c_all.shape[0]
    nt = 3
    return pl.pallas_call(
        _ada_kernel,
        grid=(DEPTH, nt),
        in_specs=[
            pl.BlockSpec((n, D_MODEL), lambda l, j: (0, 0)),
            pl.BlockSpec((1, D_MODEL, D_MODEL), lambda l, j: (l, 0, j)),
            pl.BlockSpec((1, 1, D_MODEL), lambda l, j: (l, 0, j)),
        ],
        out_specs=pl.BlockSpec((1, n, D_MODEL), lambda l, j: (l, 0, j)),
        out_shape=jax.ShapeDtypeStruct((DEPTH, n, 3 * D_MODEL), F32),
        name="ada_modulation",
    )(c_all, ada_w, ada_b.reshape(DEPTH, 1, 3 * D_MODEL))


def _interleave(first, second):
    i = j = 0
    while i < len(first) or j < len(second):
        if j >= len(second) or (i < len(first) and i * len(second) <= j * len(first)):
            first[i]()
            i += 1
        else:
            second[j]()
            j += 1


def _layer_kernel(*refs, BB, TT, NSUB, L, C, pos0, has_state, emit_vn):
    refs = list(refs)
    x_ref, mod_ref = refs[:2]
    del refs[:2]
    if has_state:
        hist0_ref, s0_ref = refs[:2]
        del refs[:2]
    (pre_g_ref, post_g_ref, w_in_ref, pool_w_ref, pool_scale_ref, sgu_g_ref, sgu_w_ref, sgu_b_ref,
     wa2_ref, ba_ref, gla_g_ref, w_o_ref) = refs[:12]
    del refs[:12]
    y_ref, hist_ref, sfin_ref = refs[:3]
    del refs[:3]
    if emit_vn:
        vn_ref = refs.pop(0)
    hb_ref, ext_ref, q_ref, k_ref, v_ref, la_ref, o_ref, st_ref = refs

    M = BB * TT
    R = M // NSUB
    t = pl.program_id(1)
    nt = pl.num_programs(1)

    @pl.when(t == 0)
    def _():
        if has_state:
            ext_ref[:, 0:HIST_ROWS, :] = hist0_ref[...]
            for bi in range(BB):
                for h in range(GLA_HEADS):
                    st_ref[bi, h] = s0_ref[bi, h].T
        else:
            ext_ref[:, 0:HIST_ROWS, :] = jnp.zeros((BB, HIST_ROWS, D_A), F32)
            st_ref[...] = jnp.zeros(st_ref.shape, F32)

    pre_g = pre_g_ref[0]
    post_g = post_g_ref[0]
    gla_g = gla_g_ref[0]
    ri = lax.broadcasted_iota(jnp.int32, (L, L), 0)
    ci_ = lax.broadcasted_iota(jnp.int32, (L, L), 1)
    wms = [jnp.where(ri >= ci_, sgu_w_ref[0, g, 0:L, 0:L], 0.0).astype(BF16) for g in range(SGU_GROUPS)]
    sgu_bias = sgu_b_ref[0, 0:L, :]
    wa2 = jnp.concatenate([wa2_ref[0], jnp.zeros((LANE - GLA_RANK, D_CK), F32)], axis=0).astype(BF16)
    cr = lax.broadcasted_iota(jnp.int32, (C, C), 0)
    cc = lax.broadcasted_iota(jnp.int32, (C, C), 1)
    tri = (cr >= cc).astype(BF16)
    sub_shift = GLA_SUB.bit_length() - 1
    mask_diag = (cr >= cc) & ((cr >> sub_shift) == (cc >> sub_shift))
    nb = C // GLA_SUB
    levels = []
    s = 1
    while 2 * s <= nb:
        sh = sub_shift + s.bit_length()
        levels.append((s, None if 2 * s == nb else (cr >> sh) == (cc >> sh)))
        s *= 2
    chunks_per_seq = TT // C

    def blk(x, i):
        return x[i * GLA_SUB:(i + 1) * GLA_SUB]

    def rows16(r):
        return jnp.broadcast_to(r, (GLA_SUB, r.shape[1]))

    def stack(blocks):
        zero = jnp.zeros((GLA_SUB, GLA_DK), F32)
        return jnp.concatenate([zero if b is None else b for b in blocks], axis=0).astype(BF16)

    def segments(sub):
        if BB == 1:
            return [(0, sub * R, R)]
        return [(bi, 0, TT) for bi in range(BB)]

    def projection_stages(sub, z):
        g0 = sub * R
        rows = slice(g0, g0 + R)

        def norm():
            for idx, (bi, s0, n) in enumerate(segments(sub)):
                xb = x_ref[bi, s0:s0 + n, :]
                shift = mod_ref[0, bi, 0:1, :]
                scale = mod_ref[0, bi, 1:2, :]
                hmod = _rms(xb, pre_g) * (1.0 + scale) + shift
                hb_ref[g0 + idx * n:g0 + (idx + 1) * n, :] = hmod.astype(BF16)

        def proj(off, width):
            return _dot(hb_ref[rows, :], w_in_ref[0, :, off:off + width])

        def keep(name, off, width):
            def f():
                z[name] = proj(off, width)
            return f

        def proj_a():
            a = proj(OFF_A, D_A)
            z["a"] = a
            for idx, (bi, s0, n) in enumerate(segments(sub)):
                ext_ref[bi, HIST_ROWS + s0:HIST_ROWS + s0 + n, :] = a[idx * n:(idx + 1) * n]

        def proj_q():
            q_ref[rows, :] = proj(OFF_Q, D_CK) * (GLA_DK ** -0.5)

        def proj_k():
            k_ref[rows, :] = proj(OFF_K, D_CK)

        def proj_v():
            v_ref[rows, :] = proj(OFF_VC, D_CV).astype(BF16)

        return [norm, proj_a, keep("ga", OFF_GA, D_A), keep("gm0", OFF_GM, D_MODEL),
                keep("vb", OFF_VB, D_B), keep("u", OFF_U, D_B), keep("gb", OFF_GB, D_B),
                keep("gm1", OFF_GM + D_MODEL, D_MODEL), keep("lr", OFF_LR, LANE),
                proj_q, proj_k, proj_v, keep("gc", OFF_GC, D_CV),
                keep("gm2", OFF_GM + 2 * D_MODEL, D_MODEL)]

    def branch_stages(sub, z):
        g0 = sub * R
        rows = slice(g0, g0 + R)
        segs = segments(sub)
        c_first = g0 // C
        chunk_ids = list(range(c_first, c_first + R // C))
        w = {}

        def pool():
            parts = []
            for idx, (bi, s0, n) in enumerate(segs):
                ab = z["a"][idx * n:(idx + 1) * n]
                row = lax.broadcasted_iota(jnp.int32, (n, POOL_GW), 0)
                pos1 = row + (t * TT + s0 + pos0 + 1)
                groups = []
                for gi, win in enumerate(POOL_WINDOWS):
                    ls = slice(gi * POOL_GW, (gi + 1) * POOL_GW)
                    base = HIST_ROWS + s0
                    terms = [ext_ref[bi, base - k:base - k + n, ls] for k in range(win)]
                    while len(terms) > 1:
                        terms = [terms[i] + terms[i + 1] for i in range(0, len(terms), 2)]
                    cnt = jnp.minimum(pos1, win).astype(F32)
                    dgi = terms[0] / cnt - ab[:, ls]
                    groups.append(_dot(dgi.astype(BF16), pool_w_ref[0, gi].astype(BF16)))
                parts.append(jnp.concatenate(groups, axis=-1))
            w["ya"] = parts[0] if len(parts) == 1 else jnp.concatenate(parts, axis=0)

        def merge_a():
            y_a = w.pop("ya") * pool_scale_ref[0] * _silu(z.pop("ga"))
            w["merged"] = (_sigmoid(z.pop("gm0"))
                           * _dot(y_a.astype(BF16), w_o_ref[0, ROW_OA:ROW_OA + D_A, :]))

        def sgu_norm():
            vb = z.pop("vb")
            mu = jnp.mean(vb, axis=-1, keepdims=True)
            vc = vb - mu
            vn = vc * lax.rsqrt(jnp.mean(vc * vc, axis=-1, keepdims=True) + EPS) * sgu_g_ref[0]
            if emit_vn:
                for idx, (bi, s0, n) in enumerate(segs):
                    vn_ref[bi, s0:s0 + n, :] = vn[idx * n:(idx + 1) * n]
            w["vnb"] = vn.astype(BF16)

        def sgu_mix():
            vnb = w.pop("vnb")
            s_rows = []
            for c0 in range(0, R, L):
                parts = [_dot(wms[g], vnb[c0:c0 + L, g * SGU_GW:(g + 1) * SGU_GW])
                         for g in range(SGU_GROUPS)]
                s_rows.append(jnp.concatenate(parts, axis=-1) + sgu_bias)
            s_all = s_rows[0] if len(s_rows) == 1 else jnp.concatenate(s_rows, axis=0)
            w["yb"] = z.pop("u") * s_all * _silu(z.pop("gb"))

        def merge_b():
            w["merged"] = w["merged"] + (_sigmoid(z.pop("gm1"))
                                         * _dot(w.pop("yb").astype(BF16), w_o_ref[0, ROW_OB:ROW_OB + D_B, :]))

        def log_decay():
            pre = _dot(z.pop("lr").astype(BF16), wa2) + ba_ref[0]
            log_sig = jnp.minimum(pre, 0.0) - jnp.log1p(jnp.exp(-jnp.abs(pre)))
            la_ref[rows, :] = log_sig * (LOG2E / GLA_NORMALIZER)

        def cumulative(ci):
            def f():
                la = la_ref[ci * C:(ci + 1) * C, :]
                p0 = la.astype(BF16)
                r1 = la - p0.astype(F32)
                p1 = r1.astype(BF16)
                p2 = (r1 - p1.astype(F32)).astype(BF16)
                w["b", ci] = _dot(tri, p0) + _dot(tri, p1) + _dot(tri, p2)
            return f

        def operands(ci):
            def f():
                b_all = w.pop(("b", ci))
                for h in range(GLA_HEADS):
                    ks = slice(h * GLA_DK, (h + 1) * GLA_DK)
                    b = b_all[:, ks]
                    qh = q_ref[ci * C:(ci + 1) * C, ks]
                    kh = k_ref[ci * C:(ci + 1) * C, ks]
                    ends = [b[(i + 1) * GLA_SUB - 1:(i + 1) * GLA_SUB, :] for i in range(nb)]
                    starts = [None] + ends[:-1]
                    ref_start = jnp.concatenate([jnp.zeros((GLA_SUB, GLA_DK), F32)]
                                                + [rows16(e) for e in ends[:-1]], axis=0)
                    ref_end = jnp.concatenate([rows16(e) for e in ends], axis=0)
                    rel = b - ref_start
                    qd = qh * jnp.exp2(rel)
                    kd = kh * jnp.exp2(-rel)
                    ke = kh * jnp.exp2(ref_end - b)

                    def q_from(i, j):
                        if j == i - 1:
                            return blk(qd, i)
                        return blk(qd, i) * rows16(jnp.exp2(starts[i] - ends[j]))

                    def k_to(i, j):
                        if j == i:
                            return blk(ke, i)
                        return blk(ke, i) * rows16(jnp.exp2(ends[j] - ends[i]))

                    pairs = [(qd.astype(BF16), kd.astype(BF16))]
                    for (sz, _) in levels:
                        qblocks, kblocks = [], []
                        for i in range(nb):
                            mid = (i // (2 * sz)) * 2 * sz + sz
                            if (i // sz) % 2 == 1:
                                qblocks.append(q_from(i, mid - 1))
                                kblocks.append(None)
                            else:
                                qblocks.append(None)
                                kblocks.append(k_to(i, mid - 1))
                        pairs.append((stack(qblocks), stack(kblocks)))
                    q_in = stack([blk(qd, 0)]
                                 + [blk(qd, i) * rows16(jnp.exp2(starts[i])) for i in range(1, nb)])
                    k_dec = stack([k_to(i, nb - 1) for i in range(nb)])
                    w["ops", ci, h] = (q_in, k_dec, jnp.exp2(ends[nb - 1]), pairs)
            return f

        def attention(ci):
            def f():
                for h in range(GLA_HEADS):
                    q_in, k_dec, dec, pairs = w.pop(("ops", ci, h))
                    att = jnp.where(mask_diag, _dot_nt(*pairs[0]), 0.0)
                    for (_, same_blk), pr in zip(levels, pairs[1:]):
                        part = _dot_nt(*pr)
                        att = att + (part if same_blk is None else jnp.where(same_blk, part, 0.0))
                    w["att", ci, h] = (q_in, k_dec, dec, att.astype(BF16))
            return f

        def state_chain(ci):
            def f():
                bi = ci // chunks_per_seq
                for h in range(GLA_HEADS):
                    q_in, k_dec, dec, att = w.pop(("att", ci, h))
                    vs = slice(h * GLA_DV, (h + 1) * GLA_DV)
                    vh = v_ref[ci * C:(ci + 1) * C, vs]
                    st = st_ref[bi, h]
                    o_ref[ci * C:(ci + 1) * C, vs] = _dot(att, vh) + _dot_nt(q_in, st.astype(BF16))
                    st_ref[bi, h] = st * dec + _dot_tn(vh, k_dec)
            return f

        def merge_c():
            parts = [_rms(o_ref[rows, h * GLA_DV:(h + 1) * GLA_DV], gla_g) for h in range(GLA_HEADS)]
            y_c = jnp.concatenate(parts, axis=-1) * _silu(z.pop("gc"))
            w["merged"] = w["merged"] + (_sigmoid(z.pop("gm2"))
                                         * _dot(y_c.astype(BF16), w_o_ref[0, ROW_OC:ROW_OC + D_CV, :]))

        def out_proj():
            w["out"] = _dot(w.pop("merged").astype(BF16), w_o_ref[0, ROW_OUT:ROW_OUT + D_MODEL, :])

        def residual():
            out = w.pop("out")
            for idx, (bi, s0, n) in enumerate(segs):
                gate = mod_ref[0, bi, 2:3, :]
                y_ref[bi, s0:s0 + n, :] = (x_ref[bi, s0:s0 + n, :]
                                           + gate * _rms(out[idx * n:(idx + 1) * n], post_g))

        stages = [pool, merge_a, sgu_norm, sgu_mix, merge_b, log_decay]
        stages += [cumulative(ci) for ci in chunk_ids]
        stages += [operands(ci) for ci in chunk_ids]
        stages += [attention(ci) for ci in chunk_ids]
        stages += [state_chain(ci) for ci in chunk_ids]
        stages += [merge_c, out_proj, residual]
        return stages

    zs = [dict() for _ in range(NSUB)]
    for st_fn in projection_stages(0, zs[0]):
        st_fn()
    for sub in range(NSUB):
        nxt = projection_stages(sub + 1, zs[sub + 1]) if sub + 1 < NSUB else []
        _interleave(branch_stages(sub, zs[sub]), nxt)

    for bi in range(BB):
        tail = ext_ref[bi, TT:TT + HIST_ROWS, :]
        ext_ref[bi, 0:HIST_ROWS, :] = tail
        hist_ref[bi] = tail

    @pl.when(t == nt - 1)
    def _():
        for bi in range(BB):
            for h in range(GLA_HEADS):
                sfin_ref[bi, h] = st_ref[bi, h].T


def _layer_tiles(B, T):
    if T >= 512:
        BB, TT, NSUB = 1, 512, 2
    else:
        BB, TT, NSUB = min(B, 256 // T), T, 1
    L = min(T, SGU_LEN)
    C = 64
    R = BB * TT // NSUB
    assert T % TT == 0 and B % BB == 0 and TT >= HIST_ROWS
    assert (NSUB == 1 or BB == 1) and R % L == 0 and R % C == 0 and TT % C == 0
    return BB, TT, NSUB, L, C


def _vmem_limit(BB, TT, weights, has_state):
    M = BB * TT
    wbytes = sum(math.prod(w.shape[1:]) * w.dtype.itemsize for w in weights)
    io = 2 * 2 * M * D_MODEL * 4
    state = 2 * (2 if has_state else 1) * BB * (GLA_HEADS * GLA_DK * GLA_DV + HIST_ROWS * D_A) * 4
    scratch = M * (D_MODEL * 2 + 3 * D_CK * 4 + D_CV * 2 + D_CV * 4) + BB * (TT + HIST_ROWS) * D_A * 4 \
        + BB * GLA_HEADS * GLA_DK * GLA_DV * 4
    temps = M * (3 * D_MODEL + 6 * D_MODEL) * 4
    return min(wbytes + io + state + scratch + temps, V7X_VMEM_BYTES - 4 * 1024 * 1024)


def _mixer_layer(l, x, mod, mod_row0, state, weights, pos0, emit_vn):
    B, T, D = x.shape
    BB, TT, NSUB, L, C = _layer_tiles(B, T)
    M = BB * TT
    has_state = state is not None
    mod_blk0 = mod_row0 // BB

    def wspec(w):
        nd = w.ndim
        return pl.BlockSpec((1,) + w.shape[1:], lambda b, t: (l,) + (0,) * (nd - 1),
                            pipeline_mode=pl.Buffered(1))

    grid = (B // BB, T // TT)
    inputs = [x, mod]
    in_specs = [
        pl.BlockSpec((BB, TT, D), lambda b, t: (b, t, 0)),
        pl.BlockSpec((1, BB, 3, D), lambda b, t: (l, mod_blk0 + b, 0, 0)),
    ]
    if has_state:
        inputs += list(state)
        in_specs += [
            pl.BlockSpec((BB, HIST_ROWS, D_A), lambda b, t: (b, 0, 0)),
            pl.BlockSpec((BB, GLA_HEADS, GLA_DK, GLA_DV), lambda b, t: (b, 0, 0, 0)),
        ]
    inputs += list(weights)
    in_specs += [wspec(w) for w in weights]
    out_shape = [
        jax.ShapeDtypeStruct((B, T, D), F32),
        jax.ShapeDtypeStruct((B, HIST_ROWS, D_A), F32),
        jax.ShapeDtypeStruct((B, GLA_HEADS, GLA_DK, GLA_DV), F32),
    ]
    out_specs = [
        pl.BlockSpec((BB, TT, D), lambda b, t: (b, t, 0)),
        pl.BlockSpec((BB, HIST_ROWS, D_A), lambda b, t: (b, 0, 0)),
        pl.BlockSpec((BB, GLA_HEADS, GLA_DK, GLA_DV), lambda b, t: (b, 0, 0, 0)),
    ]
    if emit_vn:
        out_shape.append(jax.ShapeDtypeStruct((B, T, D_B), F32))
        out_specs.append(pl.BlockSpec((BB, TT, D_B), lambda b, t: (b, t, 0)))
    scratch_shapes = [
        pltpu.VMEM((M, D), BF16),
        pltpu.VMEM((BB, HIST_ROWS + TT, D_A), F32),
        pltpu.VMEM((M, D_CK), F32),
        pltpu.VMEM((M, D_CK), F32),
        pltpu.VMEM((M, D_CV), BF16),
        pltpu.VMEM((M, D_CK), F32),
        pltpu.VMEM((M, D_CV), F32),
        pltpu.VMEM((BB, GLA_HEADS, GLA_DV, GLA_DK), F32),
    ]
    body = functools.partial(_layer_kernel, BB=BB, TT=TT, NSUB=NSUB, L=L, C=C, pos0=pos0,
                             has_state=has_state, emit_vn=emit_vn)
    return pl.pallas_call(
        body,
        grid=grid,
        in_specs=in_specs,
        out_specs=out_specs,
        out_shape=out_shape,
        scratch_shapes=scratch_shapes,
        compiler_params=pltpu.CompilerParams(
            dimension_semantics=("arbitrary", "arbitrary"),
            vmem_limit_bytes=_vmem_limit(BB, TT, weights, has_state)),
        name="mixer_layer_T%d" % T,
    )(*inputs)


def kernel(x_prompt, x_sample, state_pool, state_gla, c_prompt, c_sample, ada_w, ada_b, pre_norm_g,
           post_norm_g, w_in, pool_w, pool_scale, sgu_norm_g, sgu_w, sgu_b, gla_wa2, gla_ba, gla_norm_g,
           w_oa, w_ob, w_oc, w_out):
    bp = x_prompt.shape[0]
    bs = x_sample.shape[0]
    mod = _ada_modulation(jnp.concatenate([c_prompt, c_sample], axis=0), ada_w, ada_b)
    mod = mod.reshape(DEPTH, bp + bs, 3, D_MODEL)

    lr0 = OFF_LR
    gm0 = OFF_LR + GLA_RANK
    w_pack = jnp.concatenate(
        [w_in[..., :lr0].astype(BF16),
         jnp.pad(w_in[..., lr0:gm0].astype(BF16), ((0, 0), (0, 0), (0, LANE - GLA_RANK))),
         w_in[..., gm0:].astype(BF16)], axis=-1)
    w_o = jnp.concatenate([w_oa, w_ob, w_oc, w_out], axis=1).astype(BF16)
    sgu_bias = jnp.repeat(jnp.swapaxes(sgu_b, 1, 2), SGU_GW, axis=2)
    vec = lambda v: v[:, None, :]
    weights = [vec(pre_norm_g), vec(post_norm_g), w_pack, pool_w, vec(pool_scale), vec(sgu_norm_g),
               sgu_w, sgu_bias, gla_wa2, vec(gla_ba), vec(gla_norm_g), w_o]
    hist_sample = jnp.pad(state_pool, ((0, 0), (0, 0), (HIST_ROWS - POOL_HIST, 0), (0, 0)))

    xp, xs = x_prompt, x_sample
    pool_p, gla_p, pool_s, gla_s, sgu_s = [], [], [], [], []
    for l in range(DEPTH):
        xp, hp, sp = _mixer_layer(l, xp, mod, 0, None, weights, 0, False)
        xs, hs, ss, vs = _mixer_layer(l, xs, mod, bp, (hist_sample[l], state_gla[l]), weights,
                                      PAST_LEN, True)
        pool_p.append(hp[:, HIST_ROWS - POOL_HIST:])
        gla_p.append(sp)
        pool_s.append(hs[:, HIST_ROWS - POOL_HIST:])
        gla_s.append(ss)
        sgu_s.append(vs)
    return (xp, xs, jnp.stack(pool_p), jnp.stack(gla_p), jnp.stack(pool_s), jnp.stack(gla_s),
            jnp.stack(sgu_s))
```

```python
import functools
import math

import jax
import jax.numpy as jnp
from jax import lax
from jax.experimental import pallas as pl
from jax.experimental.pallas import tpu as pltpu

D_MODEL = 1024
DEPTH = 2
PAST_LEN = 2048
EPS = 1e-6
POOL_WINDOWS = (2, 4, 8, 16)
POOL_GW = 128
D_A = 512
POOL_HIST = 15
HIST_ROWS = 16
SGU_LEN = 128
SGU_GROUPS = 4
D_B = 512
SGU_GW = 128
GLA_HEADS = 4
GLA_DK = 128
GLA_DV = 256
D_CK = 512
D_CV = 1024
GLA_RANK = 16
GLA_NORMALIZER = 16.0
GLA_SUB = 16
LANE = 128
V7X_VMEM_BYTES = 64 * 1024 * 1024
LOG2E = math.log2(math.e)

OFF_A, OFF_GA, OFF_U, OFF_VB, OFF_GB = 0, 512, 1024, 1536, 2048
OFF_Q, OFF_K, OFF_VC, OFF_GC = 2560, 3072, 3584, 4608
OFF_LR = 5632
OFF_GM = OFF_LR + LANE
D_PACK = OFF_GM + 3 * D_MODEL
D_IN = OFF_LR + GLA_RANK + 3 * D_MODEL
PACK_ROWS = 128

F32 = jnp.float32
BF16 = jnp.bfloat16


def _dot(a, b):
    return jnp.dot(a, b, preferred_element_type=F32)


def _dot_nt(a, b):
    return lax.dot_general(a, b, (((1,), (1,)), ((), ())), preferred_element_type=F32)


def _dot_tn(a, b):
    return lax.dot_general(a, b, (((0,), (0,)), ((), ())), preferred_element_type=F32)


def _sigmoid(x):
    return 1.0 / (1.0 + jnp.exp(-x))


def _silu(x):
    return x * _sigmoid(x)


def _rms(x, g):
    return x * lax.rsqrt(jnp.mean(x * x, axis=-1, keepdims=True) + EPS) * g


def _ada_kernel(c_ref, w_ref, b_ref, o_ref):
    c = c_ref[...]
    o_ref[0] = _dot(_silu(c).astype(BF16), w_ref[0].astype(BF16)) + b_ref[0]


def _ada_modulation(c_all, ada_w, ada_b):
    n = c_all.shape[0]
    nt = 3
    return pl.pallas_call(
        _ada_kernel,
        grid=(DEPTH, nt),
        in_specs=[
            pl.BlockSpec((n, D_MODEL), lambda l, j: (0, 0)),
            pl.BlockSpec((1, D_MODEL, D_MODEL), lambda l, j: (l, 0, j)),
            pl.BlockSpec((1, 1, D_MODEL), lambda l, j: (l, 0, j)),
        ],
        out_specs=pl.BlockSpec((1, n, D_MODEL), lambda l, j: (l, 0, j)),
        out_shape=jax.ShapeDtypeStruct((DEPTH, n, 3 * D_MODEL), F32),
        name="ada_modulation",
    )(c_all, ada_w, ada_b.reshape(DEPTH, 1, 3 * D_MODEL))


def _pack_kernel(w_ref, o_ref):
    w = w_ref[0]
    o_ref[0, :, 0:OFF_LR] = w[:, 0:OFF_LR].astype(BF16)
    lr = jnp.concatenate([w[:, OFF_LR:OFF_LR + GLA_RANK],
                          jnp.zeros((PACK_ROWS, LANE - GLA_RANK), F32)], axis=1)
    o_ref[0, :, OFF_LR:OFF_GM] = lr.astype(BF16)
    o_ref[0, :, OFF_GM:D_PACK] = w[:, OFF_LR + GLA_RANK:D_IN].astype(BF16)


def _pack_w_in(w_in):
    return pl.pallas_call(
        _pack_kernel,
        grid=(DEPTH, D_MODEL // PACK_ROWS),
        in_specs=[pl.BlockSpec((1, PACK_ROWS, D_IN), lambda l, r: (l, r, 0))],
        out_specs=pl.BlockSpec((1, PACK_ROWS, D_PACK), lambda l, r: (l, r, 0)),
        out_shape=jax.ShapeDtypeStruct((DEPTH, D_MODEL, D_PACK), BF16),
        name="pack_w_in",
    )(w_in)


def _interleave(first, second):
    i = j = 0
    while i < len(first) or j < len(second):
        if j >= len(second) or (i < len(first) and i * len(second) <= j * len(first)):
            first[i]()
            i += 1
        else:
            second[j]()
            j += 1


def _layer_kernel(*refs, BB, TT, NSUB, L, C, pos0, has_state, emit_vn):
    refs = list(refs)
    x_ref, mod_ref = refs[:2]
    del refs[:2]
    if has_state:
        hist0_ref, s0_ref = refs[:2]
        del refs[:2]
    (pre_g_ref, post_g_ref, w_in_ref, pool_w_ref, pool_scale_ref, sgu_g_ref, sgu_w_ref, sgu_b_ref,
     wa2_ref, ba_ref, gla_g_ref, w_oa_ref, w_ob_ref, w_oc_ref, w_out_ref) = refs[:15]
    del refs[:15]
    y_ref, hist_ref, sfin_ref = refs[:3]
    del refs[:3]
    if emit_vn:
        vn_ref = refs.pop(0)
    hb_ref, ext_ref, q_ref, k_ref, v_ref, la_ref, o_ref, st_ref = refs

    M = BB * TT
    R = M // NSUB
    t = pl.program_id(1)
    nt = pl.num_programs(1)

    @pl.when(t == 0)
    def _():
        if has_state:
            ext_ref[:, 0:HIST_ROWS, :] = hist0_ref[0]
            for bi in range(BB):
                for h in range(GLA_HEADS):
                    st_ref[bi, h] = s0_ref[0, bi, h].T
        else:
            ext_ref[:, 0:HIST_ROWS, :] = jnp.zeros((BB, HIST_ROWS, D_A), F32)
            st_ref[...] = jnp.zeros(st_ref.shape, F32)

    pre_g = pre_g_ref[0]
    post_g = post_g_ref[0]
    gla_g = gla_g_ref[0]
    ri = lax.broadcasted_iota(jnp.int32, (L, L), 0)
    ci_ = lax.broadcasted_iota(jnp.int32, (L, L), 1)
    wms = [jnp.where(ri >= ci_, sgu_w_ref[0, g, 0:L, 0:L], 0.0).astype(BF16) for g in range(SGU_GROUPS)]
    sgu_bias = sgu_b_ref[0, 0:L, :]
    wa2 = jnp.concatenate([wa2_ref[0], jnp.zeros((LANE - GLA_RANK, D_CK), F32)], axis=0).astype(BF16)
    cr = lax.broadcasted_iota(jnp.int32, (C, C), 0)
    cc = lax.broadcasted_iota(jnp.int32, (C, C), 1)
    tri = (cr >= cc).astype(BF16)
    sub_shift = GLA_SUB.bit_length() - 1
    mask_diag = (cr >= cc) & ((cr >> sub_shift) == (cc >> sub_shift))
    nb = C // GLA_SUB
    levels = []
    s = 1
    while 2 * s <= nb:
        sh = sub_shift + s.bit_length()
        levels.append((s, None if 2 * s == nb else (cr >> sh) == (cc >> sh)))
        s *= 2
    chunks_per_seq = TT // C

    def blk(x, i):
        return x[i * GLA_SUB:(i + 1) * GLA_SUB]

    def rows16(r):
        return jnp.broadcast_to(r, (GLA_SUB, r.shape[1]))

    def stack(blocks):
        zero = jnp.zeros((GLA_SUB, GLA_DK), F32)
        return jnp.concatenate([zero if b is None else b for b in blocks], axis=0).astype(BF16)

    def segments(sub):
        if BB == 1:
            return [(0, sub * R, R)]
        return [(bi, 0, TT) for bi in range(BB)]

    def projection_stages(sub, z):
        g0 = sub * R
        rows = slice(g0, g0 + R)

        def norm():
            for idx, (bi, s0, n) in enumerate(segments(sub)):
                xb = x_ref[bi, s0:s0 + n, :]
                shift = mod_ref[0, bi, 0:1, :]
                scale = mod_ref[0, bi, 1:2, :]
                hmod = _rms(xb, pre_g) * (1.0 + scale) + shift
                hb_ref[g0 + idx * n:g0 + (idx + 1) * n, :] = hmod.astype(BF16)

        def proj(off, width):
            return _dot(hb_ref[rows, :], w_in_ref[0, :, off:off + width])

        def keep(name, off, width):
            def f():
                z[name] = proj(off, width)
            return f

        def proj_a():
            a = proj(OFF_A, D_A)
            z["a"] = a
            for idx, (bi, s0, n) in enumerate(segments(sub)):
                ext_ref[bi, HIST_ROWS + s0:HIST_ROWS + s0 + n, :] = a[idx * n:(idx + 1) * n]

        def proj_q():
            q_ref[rows, :] = proj(OFF_Q, D_CK) * (GLA_DK ** -0.5)

        def proj_k():
            k_ref[rows, :] = proj(OFF_K, D_CK)

        def proj_v():
            v_ref[rows, :] = proj(OFF_VC, D_CV).astype(BF16)

        return [norm, proj_a, keep("ga", OFF_GA, D_A), keep("gm0", OFF_GM, D_MODEL),
                keep("vb", OFF_VB, D_B), keep("u", OFF_U, D_B), keep("gb", OFF_GB, D_B),
                keep("gm1", OFF_GM + D_MODEL, D_MODEL), keep("lr", OFF_LR, LANE),
                proj_q, proj_k, proj_v, keep("gc", OFF_GC, D_CV),
                keep("gm2", OFF_GM + 2 * D_MODEL, D_MODEL)]

    def branch_stages(sub, z):
        g0 = sub * R
        rows = slice(g0, g0 + R)
        segs = segments(sub)
        c_first = g0 // C
        chunk_ids = list(range(c_first, c_first + R // C))
        w = {}

        def pool():
            parts = []
            for idx, (bi, s0, n) in enumerate(segs):
                ab = z["a"][idx * n:(idx + 1) * n]
                row = lax.broadcasted_iota(jnp.int32, (n, POOL_GW), 0)
                pos1 = row + (t * TT + s0 + pos0 + 1)
                groups = []
                for gi, win in enumerate(POOL_WINDOWS):
                    ls = slice(gi * POOL_GW, (gi + 1) * POOL_GW)
                    base = HIST_ROWS + s0
                    terms = [ext_ref[bi, base - k:base - k + n, ls] for k in range(win)]
                    while len(terms) > 1:
                        terms = [terms[i] + terms[i + 1] for i in range(0, len(terms), 2)]
                    cnt = jnp.minimum(pos1, win).astype(F32)
                    dgi = terms[0] / cnt - ab[:, ls]
                    groups.append(_dot(dgi.astype(BF16), pool_w_ref[0, gi].astype(BF16)))
                parts.append(jnp.concatenate(groups, axis=-1))
            w["ya"] = parts[0] if len(parts) == 1 else jnp.concatenate(parts, axis=0)

        def merge_a():
            y_a = w.pop("ya") * pool_scale_ref[0] * _silu(z.pop("ga"))
            w["merged"] = (_sigmoid(z.pop("gm0"))
                           * _dot(y_a.astype(BF16), w_oa_ref[0]))

        def sgu_norm():
            vb = z.pop("vb")
            mu = jnp.mean(vb, axis=-1, keepdims=True)
            vc = vb - mu
            vn = vc * lax.rsqrt(jnp.mean(vc * vc, axis=-1, keepdims=True) + EPS) * sgu_g_ref[0]
            if emit_vn:
                for idx, (bi, s0, n) in enumerate(segs):
                    vn_ref[bi, s0:s0 + n, :] = vn[idx * n:(idx + 1) * n]
            w["vnb"] = vn.astype(BF16)

        def sgu_mix():
            vnb = w.pop("vnb")
            s_rows = []
            for c0 in range(0, R, L):
                parts = [_dot(wms[g], vnb[c0:c0 + L, g * SGU_GW:(g + 1) * SGU_GW])
                         for g in range(SGU_GROUPS)]
                s_rows.append(jnp.concatenate(parts, axis=-1) + sgu_bias)
            s_all = s_rows[0] if len(s_rows) == 1 else jnp.concatenate(s_rows, axis=0)
            w["yb"] = z.pop("u") * s_all * _silu(z.pop("gb"))

        def merge_b():
            w["merged"] = w["merged"] + (_sigmoid(z.pop("gm1"))
                                         * _dot(w.pop("yb").astype(BF16), w_ob_ref[0]))

        def log_decay():
            pre = _dot(z.pop("lr").astype(BF16), wa2) + ba_ref[0]
            log_sig = jnp.minimum(pre, 0.0) - jnp.log1p(jnp.exp(-jnp.abs(pre)))
            la_ref[rows, :] = log_sig * (LOG2E / GLA_NORMALIZER)

        def cumulative(ci):
            def f():
                la = la_ref[ci * C:(ci + 1) * C, :]
                p0 = la.astype(BF16)
                r1 = la - p0.astype(F32)
                p1 = r1.astype(BF16)
                p2 = (r1 - p1.astype(F32)).astype(BF16)
                w["b", ci] = _dot(tri, p0) + _dot(tri, p1) + _dot(tri, p2)
            return f

        def operands(ci):
            def f():
                b_all = w.pop(("b", ci))
                for h in range(GLA_HEADS):
                    ks = slice(h * GLA_DK, (h + 1) * GLA_DK)
                    b = b_all[:, ks]
                    qh = q_ref[ci * C:(ci + 1) * C, ks]
                    kh = k_ref[ci * C:(ci + 1) * C, ks]
                    ends = [b[(i + 1) * GLA_SUB - 1:(i + 1) * GLA_SUB, :] for i in range(nb)]
                    starts = [None] + ends[:-1]
                    ref_start = jnp.concatenate([jnp.zeros((GLA_SUB, GLA_DK), F32)]
                                                + [rows16(e) for e in ends[:-1]], axis=0)
                    ref_end = jnp.concatenate([rows16(e) for e in ends], axis=0)
                    rel = b - ref_start
                    qd = qh * jnp.exp2(rel)
                    kd = kh * jnp.exp2(-rel)
                    ke = kh * jnp.exp2(ref_end - b)

                    def q_from(i, j):
                        if j == i - 1:
                            return blk(qd, i)
                        return blk(qd, i) * rows16(jnp.exp2(starts[i] - ends[j]))

                    def k_to(i, j):
                        if j == i:
                            return blk(ke, i)
                        return blk(ke, i) * rows16(jnp.exp2(ends[j] - ends[i]))

                    pairs = [(qd.astype(BF16), kd.astype(BF16))]
                    for (sz, _) in levels:
                        qblocks, kblocks = [], []
                        for i in range(nb):
                            mid = (i // (2 * sz)) * 2 * sz + sz
                            if (i // sz) % 2 == 1:
                                qblocks.append(q_from(i, mid - 1))
                                kblocks.append(None)
                            else:
                                qblocks.append(None)
                                kblocks.append(k_to(i, mid - 1))
                        pairs.append((stack(qblocks), stack(kblocks)))
                    q_in = stack([blk(qd, 0)]
                                 + [blk(qd, i) * rows16(jnp.exp2(starts[i])) for i in range(1, nb)])
                    k_dec = stack([k_to(i, nb - 1) for i in range(nb)])
                    w["ops", ci, h] = (q_in, k_dec, jnp.exp2(ends[nb - 1]), pairs)
            return f

        def attention(ci):
            def f():
                for h in range(GLA_HEADS):
                    q_in, k_dec, dec, pairs = w.pop(("ops", ci, h))
                    att = jnp.where(mask_diag, _dot_nt(*pairs[0]), 0.0)
                    for (_, same_blk), pr in zip(levels, pairs[1:]):
                        part = _dot_nt(*pr)
                        att = att + (part if same_blk is None else jnp.where(same_blk, part, 0.0))
                    w["att", ci, h] = (q_in, k_dec, dec, att.astype(BF16))
            return f

        def state_chain(ci):
            def f():
                bi = ci // chunks_per_seq
                for h in range(GLA_HEADS):
                    q_in, k_dec, dec, att = w.pop(("att", ci, h))
                    vs = slice(h * GLA_DV, (h + 1) * GLA_DV)
                    vh = v_ref[ci * C:(ci + 1) * C, vs]
                    st = st_ref[bi, h]
                    o_ref[ci * C:(ci + 1) * C, vs] = _dot(att, vh) + _dot_nt(q_in, st.astype(BF16))
                    st_ref[bi, h] = st * dec + _dot_tn(vh, k_dec)
            return f

        def merge_c():
            parts = [_rms(o_ref[rows, h * GLA_DV:(h + 1) * GLA_DV], gla_g) for h in range(GLA_HEADS)]
            y_c = jnp.concatenate(parts, axis=-1) * _silu(z.pop("gc"))
            w["merged"] = w["merged"] + (_sigmoid(z.pop("gm2"))
                                         * _dot(y_c.astype(BF16), w_oc_ref[0]))

        def out_proj():
            w["out"] = _dot(w.pop("merged").astype(BF16), w_out_ref[0])

        def residual():
            out = w.pop("out")
            for idx, (bi, s0, n) in enumerate(segs):
                gate = mod_ref[0, bi, 2:3, :]
                y_ref[bi, s0:s0 + n, :] = (x_ref[bi, s0:s0 + n, :]
                                           + gate * _rms(out[idx * n:(idx + 1) * n], post_g))

        stages = [pool, merge_a, sgu_norm, sgu_mix, merge_b, log_decay]
        stages += [cumulative(ci) for ci in chunk_ids]
        stages += [operands(ci) for ci in chunk_ids]
        stages += [attention(ci) for ci in chunk_ids]
        stages += [state_chain(ci) for ci in chunk_ids]
        stages += [merge_c, out_proj, residual]
        return stages

    zs = [dict() for _ in range(NSUB)]
    for st_fn in projection_stages(0, zs[0]):
        st_fn()
    for sub in range(NSUB):
        nxt = projection_stages(sub + 1, zs[sub + 1]) if sub + 1 < NSUB else []
        _interleave(branch_stages(sub, zs[sub]), nxt)

    for bi in range(BB):
        tail = ext_ref[bi, TT:TT + HIST_ROWS, :]
        ext_ref[bi, 0:HIST_ROWS, :] = tail
        hist_ref[bi] = tail

    @pl.when(t == nt - 1)
    def _():
        for bi in range(BB):
            for h in range(GLA_HEADS):
                sfin_ref[bi, h] = st_ref[bi, h].T


def _layer_tiles(B, T):
    if T >= 512:
        BB, TT, NSUB = 1, 512, 2
    else:
        BB, TT, NSUB = min(B, 256 // T), T, 1
    L = min(T, SGU_LEN)
    C = 64
    R = BB * TT // NSUB
    assert T % TT == 0 and B % BB == 0 and TT >= HIST_ROWS
    assert (NSUB == 1 or BB == 1) and R % L == 0 and R % C == 0 and TT % C == 0
    return BB, TT, NSUB, L, C


def _vmem_limit(BB, TT, weights, has_state):
    M = BB * TT
    wbytes = sum(math.prod(w.shape[1:]) * w.dtype.itemsize for w in weights)
    io = 2 * 2 * M * D_MODEL * 4
    state = 2 * (2 if has_state else 1) * BB * (GLA_HEADS * GLA_DK * GLA_DV + HIST_ROWS * D_A) * 4
    scratch = M * (D_MODEL * 2 + 3 * D_CK * 4 + D_CV * 2 + D_CV * 4) + BB * (TT + HIST_ROWS) * D_A * 4 \
        + BB * GLA_HEADS * GLA_DK * GLA_DV * 4
    temps = M * (3 * D_MODEL + 6 * D_MODEL) * 4
    return min(wbytes + io + state + scratch + temps, V7X_VMEM_BYTES - 4 * 1024 * 1024)


def _mixer_layer(l, x, mod, mod_row0, state, weights, pos0, emit_vn):
    B, T, D = x.shape
    BB, TT, NSUB, L, C = _layer_tiles(B, T)
    M = BB * TT
    has_state = state is not None
    mod_blk0 = mod_row0 // BB

    def wspec(w):
        nd = w.ndim
        return pl.BlockSpec((1,) + w.shape[1:], lambda b, t: (l,) + (0,) * (nd - 1),
                            pipeline_mode=pl.Buffered(1))

    grid = (B // BB, T // TT)
    inputs = [x, mod]
    in_specs = [
        pl.BlockSpec((BB, TT, D), lambda b, t: (b, t, 0)),
        pl.BlockSpec((1, BB, 3, D), lambda b, t: (l, mod_blk0 + b, 0, 0)),
    ]
    if has_state:
        inputs += list(state)
        in_specs += [
            pl.BlockSpec((1, BB, HIST_ROWS, D_A), lambda b, t: (l, b, 0, 0)),
            pl.BlockSpec((1, BB, GLA_HEADS, GLA_DK, GLA_DV), lambda b, t: (l, b, 0, 0, 0)),
        ]
    inputs += list(weights)
    in_specs += [wspec(w) for w in weights]
    out_shape = [
        jax.ShapeDtypeStruct((B, T, D), F32),
        jax.ShapeDtypeStruct((B, HIST_ROWS, D_A), F32),
        jax.ShapeDtypeStruct((B, GLA_HEADS, GLA_DK, GLA_DV), F32),
    ]
    out_specs = [
        pl.BlockSpec((BB, TT, D), lambda b, t: (b, t, 0)),
        pl.BlockSpec((BB, HIST_ROWS, D_A), lambda b, t: (b, 0, 0)),
        pl.BlockSpec((BB, GLA_HEADS, GLA_DK, GLA_DV), lambda b, t: (b, 0, 0, 0)),
    ]
    if emit_vn:
        out_shape.append(jax.ShapeDtypeStruct((B, T, D_B), F32))
        out_specs.append(pl.BlockSpec((BB, TT, D_B), lambda b, t: (b, t, 0)))
    scratch_shapes = [
        pltpu.VMEM((M, D), BF16),
        pltpu.VMEM((BB, HIST_ROWS + TT, D_A), F32),
        pltpu.VMEM((M, D_CK), F32),
        pltpu.VMEM((M, D_CK), F32),
        pltpu.VMEM((M, D_CV), BF16),
        pltpu.VMEM((M, D_CK), F32),
        pltpu.VMEM((M, D_CV), F32),
        pltpu.VMEM((BB, GLA_HEADS, GLA_DV, GLA_DK), F32),
    ]
    body = functools.partial(_layer_kernel, BB=BB, TT=TT, NSUB=NSUB, L=L, C=C, pos0=pos0,
                             has_state=has_state, emit_vn=emit_vn)
    return pl.pallas_call(
        body,
        grid=grid,
        in_specs=in_specs,
        out_specs=out_specs,
        out_shape=out_shape,
        scratch_shapes=scratch_shapes,
        compiler_params=pltpu.CompilerParams(
            dimension_semantics=("arbitrary", "arbitrary"),
            vmem_limit_bytes=_vmem_limit(BB, TT, weights, has_state)),
        name="mixer_layer_T%d" % T,
    )(*inputs)


def kernel(x_prompt, x_sample, state_pool, state_gla, c_prompt, c_sample, ada_w, ada_b, pre_norm_g,
           post_norm_g, w_in, pool_w, pool_scale, sgu_norm_g, sgu_w, sgu_b, gla_wa2, gla_ba, gla_norm_g,
           w_oa, w_ob, w_oc, w_out):
    bp = x_prompt.shape[0]
    bs = x_sample.shape[0]
    mod = _ada_modulation(jnp.concatenate([c_prompt, c_sample], axis=0), ada_w, ada_b)
    mod = mod.reshape(DEPTH, bp + bs, 3, D_MODEL)

    w_pack = _pack_w_in(w_in)
    sgu_bias = jnp.repeat(jnp.swapaxes(sgu_b, 1, 2), SGU_GW, axis=2)
    vec = lambda v: v[:, None, :]
    weights = [vec(pre_norm_g), vec(post_norm_g), w_pack, pool_w, vec(pool_scale), vec(sgu_norm_g),
               sgu_w, sgu_bias, gla_wa2, vec(gla_ba), vec(gla_norm_g),
               w_oa.astype(BF16), w_ob.astype(BF16), w_oc.astype(BF16), w_out.astype(BF16)]
    hist_sample = jnp.pad(state_pool, ((0, 0), (0, 0), (HIST_ROWS - POOL_HIST, 0), (0, 0)))

    xp, xs = x_prompt, x_sample
    pool_p, gla_p, pool_s, gla_s, sgu_s = [], [], [], [], []
    for l in range(DEPTH):
        xp, hp, sp = _mixer_layer(l, xp, mod, 0, None, weights, 0, False)
        xs, hs, ss, vs = _mixer_layer(l, xs, mod, bp, (hist_sample, state_gla), weights, PAST_LEN, True)
        pool_p.append(hp[:, HIST_ROWS - POOL_HIST:])
        gla_p.append(sp)
        pool_s.append(hs[:, HIST_ROWS - POOL_HIST:])
        gla_s.append(ss)
        sgu_s.append(vs)
    return (xp, xs, jnp.stack(pool_p), jnp.stack(gla_p), jnp.stack(pool_s), jnp.stack(gla_s),
            jnp.stack(sgu_s))
```

```python
import functools
import math

import jax
import jax.numpy as jnp
from jax import lax
from jax.experimental import pallas as pl
from jax.experimental.pallas import tpu as pltpu

D_MODEL = 1024
DEPTH = 2
PAST_LEN = 2048
EPS = 1e-6
POOL_WINDOWS = (2, 4, 8, 16)
POOL_GW = 128
D_A = 512
POOL_HIST = 15
HIST_ROWS = 16
SGU_LEN = 128
SGU_GROUPS = 4
D_B = 512
SGU_GW = 128
GLA_HEADS = 4
GLA_DK = 128
GLA_DV = 256
D_CK = 512
D_CV = 1024
GLA_RANK = 16
GLA_NORMALIZER = 16.0
GLA_SUB = 16
LANE = 128
V7X_VMEM_BYTES = 64 * 1024 * 1024
LOG2E = math.log2(math.e)

OFF_A, OFF_GA, OFF_U, OFF_VB, OFF_GB = 0, 512, 1024, 1536, 2048
OFF_Q, OFF_K, OFF_VC, OFF_GC = 2560, 3072, 3584, 4608
OFF_LR = 5632
OFF_GM = OFF_LR + GLA_RANK

F32 = jnp.float32
BF16 = jnp.bfloat16


def _dot(a, b):
    return jnp.dot(a, b, preferred_element_type=F32)


def _dot_nt(a, b):
    return lax.dot_general(a, b, (((1,), (1,)), ((), ())), preferred_element_type=F32)


def _dot_tn(a, b):
    return lax.dot_general(a, b, (((0,), (0,)), ((), ())), preferred_element_type=F32)


def _sigmoid(x):
    return 1.0 / (1.0 + jnp.exp(-x))


def _silu(x):
    return x * _sigmoid(x)


def _rms(x, g):
    return x * lax.rsqrt(jnp.mean(x * x, axis=-1, keepdims=True) + EPS) * g


def _ada_kernel(c_ref, w_ref, b_ref, o_ref):
    c = c_ref[...]
    o_ref[0] = _dot(_silu(c).astype(BF16), w_ref[0].astype(BF16)) + b_ref[0]


def _ada_modulation(c_all, ada_w, ada_b):
    n = c_all.shape[0]
    nt = 3
    return pl.pallas_call(
        _ada_kernel,
        grid=(DEPTH, nt),
        in_specs=[
            pl.BlockSpec((n, D_MODEL), lambda l, j: (0, 0)),
            pl.BlockSpec((1, D_MODEL, D_MODEL), lambda l, j: (l, 0, j)),
            pl.BlockSpec((1, 1, D_MODEL), lambda l, j: (l, 0, j)),
        ],
        out_specs=pl.BlockSpec((1, n, D_MODEL), lambda l, j: (l, 0, j)),
        out_shape=jax.ShapeDtypeStruct((DEPTH, n, 3 * D_MODEL), F32),
        name="ada_modulation",
    )(c_all, ada_w, ada_b.reshape(DEPTH, 1, 3 * D_MODEL))


def _interleave(first, second):
    i = j = 0
    while i < len(first) or j < len(second):
        if j >= len(second) or (i < len(first) and i * len(second) <= j * len(first)):
            first[i]()
            i += 1
        else:
            second[j]()
            j += 1


def _layer_kernel(*refs, BB, TT, NSUB, L, C, pos0, has_state, emit_vn):
    refs = list(refs)
    x_ref, mod_ref = refs[:2]
    del refs[:2]
    if has_state:
        hist0_ref, s0_ref = refs[:2]
        del refs[:2]
    (pre_g_ref, post_g_ref, w_in_ref, pool_w_ref, pool_scale_ref, sgu_g_ref, sgu_w_ref, sgu_b_ref,
     wa2_ref, ba_ref, gla_g_ref, w_oa_ref, w_ob_ref, w_oc_ref, w_out_ref) = refs[:15]
    del refs[:15]
    y_ref, hist_ref, sfin_ref = refs[:3]
    del refs[:3]
    if emit_vn:
        vn_ref = refs.pop(0)
    hb_ref, ext_ref, q_ref, k_ref, v_ref, la_ref, o_ref, st_ref = refs

    M = BB * TT
    R = M // NSUB
    t = pl.program_id(1)
    nt = pl.num_programs(1)

    @pl.when(t == 0)
    def _():
        if has_state:
            ext_ref[:, 0:HIST_ROWS, :] = hist0_ref[0]
            for bi in range(BB):
                for h in range(GLA_HEADS):
                    st_ref[bi, h] = s0_ref[0, bi, h].T
        else:
            ext_ref[:, 0:HIST_ROWS, :] = jnp.zeros((BB, HIST_ROWS, D_A), F32)
            st_ref[...] = jnp.zeros(st_ref.shape, F32)

    pre_g = pre_g_ref[0]
    post_g = post_g_ref[0]
    gla_g = gla_g_ref[0]
    ri = lax.broadcasted_iota(jnp.int32, (L, L), 0)
    ci_ = lax.broadcasted_iota(jnp.int32, (L, L), 1)
    wms = [jnp.where(ri >= ci_, sgu_w_ref[0, g, 0:L, 0:L], 0.0).astype(BF16) for g in range(SGU_GROUPS)]
    sgu_bias = sgu_b_ref[0, 0:L, :]
    wa2 = jnp.concatenate([wa2_ref[0], jnp.zeros((LANE - GLA_RANK, D_CK), F32)], axis=0).astype(BF16)
    cr = lax.broadcasted_iota(jnp.int32, (C, C), 0)
    cc = lax.broadcasted_iota(jnp.int32, (C, C), 1)
    tri = (cr >= cc).astype(BF16)
    sub_shift = GLA_SUB.bit_length() - 1
    mask_diag = (cr >= cc) & ((cr >> sub_shift) == (cc >> sub_shift))
    nb = C // GLA_SUB
    levels = []
    s = 1
    while 2 * s <= nb:
        sh = sub_shift + s.bit_length()
        levels.append((s, None if 2 * s == nb else (cr >> sh) == (cc >> sh)))
        s *= 2
    chunks_per_seq = TT // C

    def blk(x, i):
        return x[i * GLA_SUB:(i + 1) * GLA_SUB]

    def rows16(r):
        return jnp.broadcast_to(r, (GLA_SUB, r.shape[1]))

    def stack(blocks):
        zero = jnp.zeros((GLA_SUB, GLA_DK), F32)
        return jnp.concatenate([zero if b is None else b for b in blocks], axis=0).astype(BF16)

    def segments(sub):
        if BB == 1:
            return [(0, sub * R, R)]
        return [(bi, 0, TT) for bi in range(BB)]

    def projection_stages(sub, z):
        g0 = sub * R
        rows = slice(g0, g0 + R)

        def norm():
            for idx, (bi, s0, n) in enumerate(segments(sub)):
                xb = x_ref[bi, s0:s0 + n, :]
                shift = mod_ref[0, bi, 0:1, :]
                scale = mod_ref[0, bi, 1:2, :]
                hmod = _rms(xb, pre_g) * (1.0 + scale) + shift
                hb_ref[g0 + idx * n:g0 + (idx + 1) * n, :] = hmod.astype(BF16)

        def proj(off, width):
            return _dot_nt(hb_ref[rows, :], w_in_ref[0, off:off + width, :])

        def keep(name, off, width):
            def f():
                z[name] = proj(off, width)
            return f

        def proj_a():
            a = proj(OFF_A, D_A)
            z["a"] = a
            for idx, (bi, s0, n) in enumerate(segments(sub)):
                ext_ref[bi, HIST_ROWS + s0:HIST_ROWS + s0 + n, :] = a[idx * n:(idx + 1) * n]

        def proj_q():
            q_ref[rows, :] = proj(OFF_Q, D_CK) * (GLA_DK ** -0.5)

        def proj_k():
            k_ref[rows, :] = proj(OFF_K, D_CK)

        def proj_v():
            v_ref[rows, :] = proj(OFF_VC, D_CV).astype(BF16)

        return [norm, proj_a, keep("ga", OFF_GA, D_A), keep("gm0", OFF_GM, D_MODEL),
                keep("vb", OFF_VB, D_B), keep("u", OFF_U, D_B), keep("gb", OFF_GB, D_B),
                keep("gm1", OFF_GM + D_MODEL, D_MODEL), keep("lr", OFF_LR, LANE),
                proj_q, proj_k, proj_v, keep("gc", OFF_GC, D_CV),
                keep("gm2", OFF_GM + 2 * D_MODEL, D_MODEL)]

    def branch_stages(sub, z):
        g0 = sub * R
        rows = slice(g0, g0 + R)
        segs = segments(sub)
        c_first = g0 // C
        chunk_ids = list(range(c_first, c_first + R // C))
        w = {}

        def pool():
            parts = []
            for idx, (bi, s0, n) in enumerate(segs):
                ab = z["a"][idx * n:(idx + 1) * n]
                row = lax.broadcasted_iota(jnp.int32, (n, POOL_GW), 0)
                pos1 = row + (t * TT + s0 + pos0 + 1)
                groups = []
                for gi, win in enumerate(POOL_WINDOWS):
                    ls = slice(gi * POOL_GW, (gi + 1) * POOL_GW)
                    base = HIST_ROWS + s0
                    terms = [ext_ref[bi, base - k:base - k + n, ls] for k in range(win)]
                    while len(terms) > 1:
                        terms = [terms[i] + terms[i + 1] for i in range(0, len(terms), 2)]
                    cnt = jnp.minimum(pos1, win).astype(F32)
                    dgi = terms[0] / cnt - ab[:, ls]
                    groups.append(_dot(dgi.astype(BF16), pool_w_ref[0, gi].astype(BF16)))
                parts.append(jnp.concatenate(groups, axis=-1))
            w["ya"] = parts[0] if len(parts) == 1 else jnp.concatenate(parts, axis=0)

        def merge_a():
            y_a = w.pop("ya") * pool_scale_ref[0] * _silu(z.pop("ga"))
            w["merged"] = (_sigmoid(z.pop("gm0"))
                           * _dot(y_a.astype(BF16), w_oa_ref[0]))

        def sgu_norm():
            vb = z.pop("vb")
            mu = jnp.mean(vb, axis=-1, keepdims=True)
            vc = vb - mu
            vn = vc * lax.rsqrt(jnp.mean(vc * vc, axis=-1, keepdims=True) + EPS) * sgu_g_ref[0]
            if emit_vn:
                for idx, (bi, s0, n) in enumerate(segs):
                    vn_ref[bi, s0:s0 + n, :] = vn[idx * n:(idx + 1) * n]
            w["vnb"] = vn.astype(BF16)

        def sgu_mix():
            vnb = w.pop("vnb")
            s_rows = []
            for c0 in range(0, R, L):
                parts = [_dot(wms[g], vnb[c0:c0 + L, g * SGU_GW:(g + 1) * SGU_GW])
                         for g in range(SGU_GROUPS)]
                s_rows.append(jnp.concatenate(parts, axis=-1) + sgu_bias)
            s_all = s_rows[0] if len(s_rows) == 1 else jnp.concatenate(s_rows, axis=0)
            w["yb"] = z.pop("u") * s_all * _silu(z.pop("gb"))

        def merge_b():
            w["merged"] = w["merged"] + (_sigmoid(z.pop("gm1"))
                                         * _dot(w.pop("yb").astype(BF16), w_ob_ref[0]))

        def log_decay():
            pre = _dot(z.pop("lr").astype(BF16), wa2) + ba_ref[0]
            log_sig = jnp.minimum(pre, 0.0) - jnp.log1p(jnp.exp(-jnp.abs(pre)))
            la_ref[rows, :] = log_sig * (LOG2E / GLA_NORMALIZER)

        def cumulative(ci):
            def f():
                la = la_ref[ci * C:(ci + 1) * C, :]
                p0 = la.astype(BF16)
                r1 = la - p0.astype(F32)
                p1 = r1.astype(BF16)
                p2 = (r1 - p1.astype(F32)).astype(BF16)
                w["b", ci] = _dot(tri, p0) + _dot(tri, p1) + _dot(tri, p2)
            return f

        def operands(ci):
            def f():
                b_all = w.pop(("b", ci))
                for h in range(GLA_HEADS):
                    ks = slice(h * GLA_DK, (h + 1) * GLA_DK)
                    b = b_all[:, ks]
                    qh = q_ref[ci * C:(ci + 1) * C, ks]
                    kh = k_ref[ci * C:(ci + 1) * C, ks]
                    ends = [b[(i + 1) * GLA_SUB - 1:(i + 1) * GLA_SUB, :] for i in range(nb)]
                    starts = [None] + ends[:-1]
                    ref_start = jnp.concatenate([jnp.zeros((GLA_SUB, GLA_DK), F32)]
                                                + [rows16(e) for e in ends[:-1]], axis=0)
                    ref_end = jnp.concatenate([rows16(e) for e in ends], axis=0)
                    rel = b - ref_start
                    qd = qh * jnp.exp2(rel)
                    kd = kh * jnp.exp2(-rel)
                    ke = kh * jnp.exp2(ref_end - b)

                    def q_from(i, j):
                        if j == i - 1:
                            return blk(qd, i)
                        return blk(qd, i) * rows16(jnp.exp2(starts[i] - ends[j]))

                    def k_to(i, j):
                        if j == i:
                            return blk(ke, i)
                        return blk(ke, i) * rows16(jnp.exp2(ends[j] - ends[i]))

                    pairs = [(qd.astype(BF16), kd.astype(BF16))]
                    for (sz, _) in levels:
                        qblocks, kblocks = [], []
                        for i in range(nb):
                            mid = (i // (2 * sz)) * 2 * sz + sz
                            if (i // sz) % 2 == 1:
                                qblocks.append(q_from(i, mid - 1))
                                kblocks.append(None)
                            else:
                                qblocks.append(None)
                                kblocks.append(k_to(i, mid - 1))
                        pairs.append((stack(qblocks), stack(kblocks)))
                    q_in = stack([blk(qd, 0)]
                                 + [blk(qd, i) * rows16(jnp.exp2(starts[i])) for i in range(1, nb)])
                    k_dec = stack([k_to(i, nb - 1) for i in range(nb)])
                    w["ops", ci, h] = (q_in, k_dec, jnp.exp2(ends[nb - 1]), pairs)
            return f

        def attention(ci):
            def f():
                for h in range(GLA_HEADS):
                    q_in, k_dec, dec, pairs = w.pop(("ops", ci, h))
                    att = jnp.where(mask_diag, _dot_nt(*pairs[0]), 0.0)
                    for (_, same_blk), pr in zip(levels, pairs[1:]):
                        part = _dot_nt(*pr)
                        att = att + (part if same_blk is None else jnp.where(same_blk, part, 0.0))
                    w["att", ci, h] = (q_in, k_dec, dec, att.astype(BF16))
            return f

        def state_chain(ci):
            def f():
                bi = ci // chunks_per_seq
                for h in range(GLA_HEADS):
                    q_in, k_dec, dec, att = w.pop(("att", ci, h))
                    vs = slice(h * GLA_DV, (h + 1) * GLA_DV)
                    vh = v_ref[ci * C:(ci + 1) * C, vs]
                    st = st_ref[bi, h]
                    o_ref[ci * C:(ci + 1) * C, vs] = _dot(att, vh) + _dot_nt(q_in, st.astype(BF16))
                    st_ref[bi, h] = st * dec + _dot_tn(vh, k_dec)
            return f

        def merge_c():
            parts = [_rms(o_ref[rows, h * GLA_DV:(h + 1) * GLA_DV], gla_g) for h in range(GLA_HEADS)]
            y_c = jnp.concatenate(parts, axis=-1) * _silu(z.pop("gc"))
            w["merged"] = w["merged"] + (_sigmoid(z.pop("gm2"))
                                         * _dot(y_c.astype(BF16), w_oc_ref[0]))

        def out_proj():
            w["out"] = _dot(w.pop("merged").astype(BF16), w_out_ref[0])

        def residual():
            out = w.pop("out")
            for idx, (bi, s0, n) in enumerate(segs):
                gate = mod_ref[0, bi, 2:3, :]
                y_ref[bi, s0:s0 + n, :] = (x_ref[bi, s0:s0 + n, :]
                                           + gate * _rms(out[idx * n:(idx + 1) * n], post_g))

        stages = [pool, merge_a, sgu_norm, sgu_mix, merge_b, log_decay]
        stages += [cumulative(ci) for ci in chunk_ids]
        stages += [operands(ci) for ci in chunk_ids]
        stages += [attention(ci) for ci in chunk_ids]
        stages += [state_chain(ci) for ci in chunk_ids]
        stages += [merge_c, out_proj, residual]
        return stages

    zs = [dict() for _ in range(NSUB)]
    for st_fn in projection_stages(0, zs[0]):
        st_fn()
    for sub in range(NSUB):
        nxt = projection_stages(sub + 1, zs[sub + 1]) if sub + 1 < NSUB else []
        _interleave(branch_stages(sub, zs[sub]), nxt)

    for bi in range(BB):
        tail = ext_ref[bi, TT:TT + HIST_ROWS, :]
        ext_ref[bi, 0:HIST_ROWS, :] = tail
        hist_ref[bi] = tail

    @pl.when(t == nt - 1)
    def _():
        for bi in range(BB):
            for h in range(GLA_HEADS):
                sfin_ref[bi, h] = st_ref[bi, h].T


def _layer_tiles(B, T):
    if T >= 512:
        BB, TT, NSUB = 1, 512, 2
    else:
        BB, TT, NSUB = min(B, 256 // T), T, 1
    L = min(T, SGU_LEN)
    C = 64
    R = BB * TT // NSUB
    assert T % TT == 0 and B % BB == 0 and TT >= HIST_ROWS
    assert (NSUB == 1 or BB == 1) and R % L == 0 and R % C == 0 and TT % C == 0
    return BB, TT, NSUB, L, C


def _vmem_limit(BB, TT, weights, has_state):
    M = BB * TT
    wbytes = sum(math.prod(w.shape[1:]) * w.dtype.itemsize for w in weights)
    io = 2 * 2 * M * D_MODEL * 4
    state = 2 * (2 if has_state else 1) * BB * (GLA_HEADS * GLA_DK * GLA_DV + HIST_ROWS * D_A) * 4
    scratch = M * (D_MODEL * 2 + 3 * D_CK * 4 + D_CV * 2 + D_CV * 4) + BB * (TT + HIST_ROWS) * D_A * 4 \
        + BB * GLA_HEADS * GLA_DK * GLA_DV * 4
    temps = M * (3 * D_MODEL + 6 * D_MODEL) * 4
    return min(wbytes + io + state + scratch + temps, V7X_VMEM_BYTES - 4 * 1024 * 1024)


def _mixer_layer(l, x, mod, mod_row0, state, weights, pos0, emit_vn):
    B, T, D = x.shape
    BB, TT, NSUB, L, C = _layer_tiles(B, T)
    M = BB * TT
    has_state = state is not None
    mod_blk0 = mod_row0 // BB

    def wspec(w):
        nd = w.ndim
        return pl.BlockSpec((1,) + w.shape[1:], lambda b, t: (l,) + (0,) * (nd - 1),
                            pipeline_mode=pl.Buffered(1))

    grid = (B // BB, T // TT)
    inputs = [x, mod]
    in_specs = [
        pl.BlockSpec((BB, TT, D), lambda b, t: (b, t, 0)),
        pl.BlockSpec((1, BB, 3, D), lambda b, t: (l, mod_blk0 + b, 0, 0)),
    ]
    if has_state:
        inputs += list(state)
        in_specs += [
            pl.BlockSpec((1, BB, HIST_ROWS, D_A), lambda b, t: (l, b, 0, 0)),
            pl.BlockSpec((1, BB, GLA_HEADS, GLA_DK, GLA_DV), lambda b, t: (l, b, 0, 0, 0)),
        ]
    inputs += list(weights)
    in_specs += [wspec(w) for w in weights]
    out_shape = [
        jax.ShapeDtypeStruct((B, T, D), F32),
        jax.ShapeDtypeStruct((B, HIST_ROWS, D_A), F32),
        jax.ShapeDtypeStruct((B, GLA_HEADS, GLA_DK, GLA_DV), F32),
    ]
    out_specs = [
        pl.BlockSpec((BB, TT, D), lambda b, t: (b, t, 0)),
        pl.BlockSpec((BB, HIST_ROWS, D_A), lambda b, t: (b, 0, 0)),
        pl.BlockSpec((BB, GLA_HEADS, GLA_DK, GLA_DV), lambda b, t: (b, 0, 0, 0)),
    ]
    if emit_vn:
        out_shape.append(jax.ShapeDtypeStruct((B, T, D_B), F32))
        out_specs.append(pl.BlockSpec((BB, TT, D_B), lambda b, t: (b, t, 0)))
    scratch_shapes = [
        pltpu.VMEM((M, D), BF16),
        pltpu.VMEM((BB, HIST_ROWS + TT, D_A), F32),
        pltpu.VMEM((M, D_CK), F32),
        pltpu.VMEM((M, D_CK), F32),
        pltpu.VMEM((M, D_CV), BF16),
        pltpu.VMEM((M, D_CK), F32),
        pltpu.VMEM((M, D_CV), F32),
        pltpu.VMEM((BB, GLA_HEADS, GLA_DV, GLA_DK), F32),
    ]
    body = functools.partial(_layer_kernel, BB=BB, TT=TT, NSUB=NSUB, L=L, C=C, pos0=pos0,
                             has_state=has_state, emit_vn=emit_vn)
    return pl.pallas_call(
        body,
        grid=grid,
        in_specs=in_specs,
        out_specs=out_specs,
        out_shape=out_shape,
        scratch_shapes=scratch_shapes,
        compiler_params=pltpu.CompilerParams(
            dimension_semantics=("arbitrary", "arbitrary"),
            vmem_limit_bytes=_vmem_limit(BB, TT, weights, has_state)),
        name="mixer_layer_T%d" % T,
    )(*inputs)


def kernel(x_prompt, x_sample, state_pool, state_gla, c_prompt, c_sample, ada_w, ada_b, pre_norm_g,
           post_norm_g, w_in, pool_w, pool_scale, sgu_norm_g, sgu_w, sgu_b, gla_wa2, gla_ba, gla_norm_g,
           w_oa, w_ob, w_oc, w_out):
    bp = x_prompt.shape[0]
    bs = x_sample.shape[0]
    mod = _ada_modulation(jnp.concatenate([c_prompt, c_sample], axis=0), ada_w, ada_b)
    mod = mod.reshape(DEPTH, bp + bs, 3, D_MODEL)

    w_pack = jnp.swapaxes(w_in, 1, 2).astype(BF16)
    sgu_bias = jnp.repeat(jnp.swapaxes(sgu_b, 1, 2), SGU_GW, axis=2)
    vec = lambda v: v[:, None, :]
    weights = [vec(pre_norm_g), vec(post_norm_g), w_pack, pool_w, vec(pool_scale), vec(sgu_norm_g),
               sgu_w, sgu_bias, gla_wa2, vec(gla_ba), vec(gla_norm_g),
               w_oa.astype(BF16), w_ob.astype(BF16), w_oc.astype(BF16), w_out.astype(BF16)]
    hist_sample = jnp.pad(state_pool, ((0, 0), (0, 0), (HIST_ROWS - POOL_HIST, 0), (0, 0)))

    xp, xs = x_prompt, x_sample
    pool_p, gla_p, pool_s, gla_s, sgu_s = [], [], [], [], []
    for l in range(DEPTH):
        xp, hp, sp = _mixer_layer(l, xp, mod, 0, None, weights, 0, False)
        xs, hs, ss, vs = _mixer_layer(l, xs, mod, bp, (hist_sample, state_gla), weights, PAST_LEN, True)
        pool_p.append(hp[:, HIST_ROWS - POOL_HIST:])
        gla_p.append(sp)
        pool_s.append(hs[:, HIST_ROWS - POOL_HIST:])
        gla_s.append(ss)
        sgu_s.append(vs)
    return (xp, xs, jnp.stack(pool_p), jnp.stack(gla_p), jnp.stack(pool_s), jnp.stack(gla_s),
            jnp.stack(sgu_s))
```

```python
import functools
import math

import jax
import jax.numpy as jnp
from jax import lax
from jax.experimental import pallas as pl
from jax.experimental.pallas import tpu as pltpu

D_MODEL = 1024
DEPTH = 2
PAST_LEN = 2048
EPS = 1e-6
POOL_WINDOWS = (2, 4, 8, 16)
POOL_GW = 128
D_A = 512
POOL_HIST = 15
HIST_ROWS = 16
SGU_LEN = 128
SGU_GROUPS = 4
D_B = 512
SGU_GW = 128
GLA_HEADS = 4
GLA_DK = 128
GLA_DV = 256
D_CK = 512
D_CV = 1024
GLA_RANK = 16
GLA_NORMALIZER = 16.0
GLA_SUB = 16
LANE = 128
V7X_VMEM_BYTES = 64 * 1024 * 1024
LOG2E = math.log2(math.e)

OFF_A, OFF_GA, OFF_U, OFF_VB, OFF_GB = 0, 512, 1024, 1536, 2048
OFF_Q, OFF_K, OFF_VC, OFF_GC = 2560, 3072, 3584, 4608
OFF_LR = 5632
OFF_GM = OFF_LR + GLA_RANK

F32 = jnp.float32
BF16 = jnp.bfloat16


def _dot(a, b):
    return jnp.dot(a, b, preferred_element_type=F32)


def _dot_nt(a, b):
    return lax.dot_general(a, b, (((1,), (1,)), ((), ())), preferred_element_type=F32)


def _dot_tn(a, b):
    return lax.dot_general(a, b, (((0,), (0,)), ((), ())), preferred_element_type=F32)


def _sigmoid(x):
    return 1.0 / (1.0 + jnp.exp2(x * (-LOG2E)))


def _silu(x):
    return x * _sigmoid(x)


def _rms(x, g):
    return x * lax.rsqrt(jnp.mean(x * x, axis=-1, keepdims=True) + EPS) * g


def _ada_kernel(c_ref, w_ref, b_ref, o_ref):
    c = c_ref[...]
    o_ref[0] = _dot(_silu(c).astype(BF16), w_ref[0].astype(BF16)) + b_ref[0]


def _ada_modulation(c_all, ada_w, ada_b):
    n = c_all.shape[0]
    nt = 3
    return pl.pallas_call(
        _ada_kernel,
        grid=(DEPTH, nt),
        in_specs=[
            pl.BlockSpec((n, D_MODEL), lambda l, j: (0, 0)),
            pl.BlockSpec((1, D_MODEL, D_MODEL), lambda l, j: (l, 0, j)),
            pl.BlockSpec((1, 1, D_MODEL), lambda l, j: (l, 0, j)),
        ],
        out_specs=pl.BlockSpec((1, n, D_MODEL), lambda l, j: (l, 0, j)),
        out_shape=jax.ShapeDtypeStruct((DEPTH, n, 3 * D_MODEL), F32),
        name="ada_modulation",
    )(c_all, ada_w, ada_b.reshape(DEPTH, 1, 3 * D_MODEL))


def _interleave(first, second):
    i = j = 0
    while i < len(first) or j < len(second):
        if j >= len(second) or (i < len(first) and i * len(second) <= j * len(first)):
            first[i]()
            i += 1
        else:
            second[j]()
            j += 1


def _layer_kernel(*refs, BB, TT, NSUB, L, C, pos0, has_state, emit_vn):
    refs = list(refs)
    x_ref, mod_ref = refs[:2]
    del refs[:2]
    if has_state:
        hist0_ref, s0_ref = refs[:2]
        del refs[:2]
    (pre_g_ref, post_g_ref, w_in_ref, pool_w_ref, pool_scale_ref, sgu_g_ref, sgu_w_ref, sgu_b_ref,
     wa2_ref, ba_ref, gla_g_ref, w_oa_ref, w_ob_ref, w_oc_ref, w_out_ref) = refs[:15]
    del refs[:15]
    y_ref, hist_ref, sfin_ref = refs[:3]
    del refs[:3]
    if emit_vn:
        vn_ref = refs.pop(0)
    hb_ref, ext_ref, q_ref, k_ref, v_ref, la_ref, o_ref, st_ref = refs

    M = BB * TT
    R = M // NSUB
    t = pl.program_id(1)
    nt = pl.num_programs(1)

    @pl.when(t == 0)
    def _():
        if has_state:
            ext_ref[:, 0:HIST_ROWS, :] = hist0_ref[0]
            st_ref[...] = s0_ref[0]
        else:
            ext_ref[:, 0:HIST_ROWS, :] = jnp.zeros((BB, HIST_ROWS, D_A), F32)
            st_ref[...] = jnp.zeros(st_ref.shape, F32)

    pre_g = pre_g_ref[0]
    post_g = post_g_ref[0]
    gla_g = gla_g_ref[0]
    ri = lax.broadcasted_iota(jnp.int32, (L, L), 0)
    ci_ = lax.broadcasted_iota(jnp.int32, (L, L), 1)
    wms = [jnp.where(ri >= ci_, sgu_w_ref[0, g, 0:L, 0:L], 0.0).astype(BF16) for g in range(SGU_GROUPS)]
    sgu_bias = sgu_b_ref[0, 0:L, :]
    wa2 = jnp.concatenate([wa2_ref[0], jnp.zeros((LANE - GLA_RANK, D_CK), F32)], axis=0).astype(BF16)
    cr = lax.broadcasted_iota(jnp.int32, (C, C), 0)
    cc = lax.broadcasted_iota(jnp.int32, (C, C), 1)
    tri = (cr >= cc).astype(BF16)
    sub_shift = GLA_SUB.bit_length() - 1
    mask_diag = (cr >= cc) & ((cr >> sub_shift) == (cc >> sub_shift))
    nb = C // GLA_SUB
    levels = []
    s = 1
    while 2 * s <= nb:
        sh = sub_shift + s.bit_length()
        levels.append((s, None if 2 * s == nb else (cr >> sh) == (cc >> sh)))
        s *= 2
    chunks_per_seq = TT // C

    def blk(x, i):
        return x[i * GLA_SUB:(i + 1) * GLA_SUB]

    def rows16(r):
        return jnp.broadcast_to(r, (GLA_SUB, r.shape[1]))

    def stack(blocks):
        zero = jnp.zeros((GLA_SUB, GLA_DK), F32)
        return jnp.concatenate([zero if b is None else b for b in blocks], axis=0).astype(BF16)

    def segments(sub):
        if BB == 1:
            return [(0, sub * R, R)]
        return [(bi, 0, TT) for bi in range(BB)]

    def projection_stages(sub, z):
        g0 = sub * R
        rows = slice(g0, g0 + R)

        def norm():
            for idx, (bi, s0, n) in enumerate(segments(sub)):
                xb = x_ref[bi, s0:s0 + n, :]
                shift = mod_ref[0, bi, 0:1, :]
                scale = mod_ref[0, bi, 1:2, :]
                hmod = _rms(xb, pre_g * (1.0 + scale)) + shift
                hb_ref[g0 + idx * n:g0 + (idx + 1) * n, :] = hmod.astype(BF16)

        def proj(off, width):
            return _dot_nt(hb_ref[rows, :], w_in_ref[0, off:off + width, :])

        def keep(name, off, width):
            def f():
                z[name] = proj(off, width)
            return f

        def proj_a():
            a = proj(OFF_A, D_A)
            z["a"] = a
            for idx, (bi, s0, n) in enumerate(segments(sub)):
                ext_ref[bi, HIST_ROWS + s0:HIST_ROWS + s0 + n, :] = a[idx * n:(idx + 1) * n]

        def proj_q():
            q_ref[rows, :] = proj(OFF_Q, D_CK) * (GLA_DK ** -0.5)

        def proj_k():
            k_ref[rows, :] = proj(OFF_K, D_CK)

        def proj_v():
            v_ref[rows, :] = proj(OFF_VC, D_CV).astype(BF16)

        return [norm, proj_a, keep("ga", OFF_GA, D_A), keep("gm0", OFF_GM, D_MODEL),
                keep("vb", OFF_VB, D_B), keep("u", OFF_U, D_B), keep("gb", OFF_GB, D_B),
                keep("gm1", OFF_GM + D_MODEL, D_MODEL), keep("lr", OFF_LR, LANE),
                proj_q, proj_k, proj_v, keep("gc", OFF_GC, D_CV),
                keep("gm2", OFF_GM + 2 * D_MODEL, D_MODEL)]

    def branch_stages(sub, z):
        g0 = sub * R
        rows = slice(g0, g0 + R)
        segs = segments(sub)
        c_first = g0 // C
        chunk_ids = list(range(c_first, c_first + R // C))
        w = {}

        def pool():
            parts = []
            for idx, (bi, s0, n) in enumerate(segs):
                ab = z["a"][idx * n:(idx + 1) * n]
                row = lax.broadcasted_iota(jnp.int32, (n, POOL_GW), 0)
                pos1 = row + (t * TT + s0 + pos0 + 1)
                groups = []
                for gi, win in enumerate(POOL_WINDOWS):
                    ls = slice(gi * POOL_GW, (gi + 1) * POOL_GW)
                    base = HIST_ROWS + s0
                    terms = [ext_ref[bi, base - k:base - k + n, ls] for k in range(win)]
                    while len(terms) > 1:
                        terms = [terms[i] + terms[i + 1] for i in range(0, len(terms), 2)]
                    cnt = jnp.minimum(pos1, win).astype(F32)
                    dgi = terms[0] / cnt - ab[:, ls]
                    groups.append(_dot(dgi.astype(BF16), pool_w_ref[0, gi].astype(BF16)))
                parts.append(jnp.concatenate(groups, axis=-1))
            w["ya"] = parts[0] if len(parts) == 1 else jnp.concatenate(parts, axis=0)

        def merge_a():
            y_a = w.pop("ya") * pool_scale_ref[0] * _silu(z.pop("ga"))
            w["merged"] = (_sigmoid(z.pop("gm0"))
                           * _dot(y_a.astype(BF16), w_oa_ref[0]))

        def sgu_norm():
            vb = z.pop("vb")
            mu = jnp.mean(vb, axis=-1, keepdims=True)
            vc = vb - mu
            vn = vc * lax.rsqrt(jnp.mean(vc * vc, axis=-1, keepdims=True) + EPS) * sgu_g_ref[0]
            if emit_vn:
                for idx, (bi, s0, n) in enumerate(segs):
                    vn_ref[bi, s0:s0 + n, :] = vn[idx * n:(idx + 1) * n]
            w["vnb"] = vn.astype(BF16)

        def sgu_mix():
            vnb = w.pop("vnb")
            s_rows = []
            for c0 in range(0, R, L):
                parts = [_dot(wms[g], vnb[c0:c0 + L, g * SGU_GW:(g + 1) * SGU_GW])
                         for g in range(SGU_GROUPS)]
                s_rows.append(jnp.concatenate(parts, axis=-1) + sgu_bias)
            s_all = s_rows[0] if len(s_rows) == 1 else jnp.concatenate(s_rows, axis=0)
            w["yb"] = z.pop("u") * s_all * _silu(z.pop("gb"))

        def merge_b():
            w["merged"] = w["merged"] + (_sigmoid(z.pop("gm1"))
                                         * _dot(w.pop("yb").astype(BF16), w_ob_ref[0]))

        def log_decay():
            pre = _dot(z.pop("lr").astype(BF16), wa2) + ba_ref[0]
            log_sig = jnp.minimum(pre, 0.0) - jnp.log(1.0 + jnp.exp2(jnp.abs(pre) * (-LOG2E)))
            la_ref[rows, :] = log_sig * (LOG2E / GLA_NORMALIZER)

        def cumulative(ci):
            def f():
                la = la_ref[ci * C:(ci + 1) * C, :]
                p0 = la.astype(BF16)
                r1 = la - p0.astype(F32)
                p1 = r1.astype(BF16)
                p2 = (r1 - p1.astype(F32)).astype(BF16)
                w["b", ci] = _dot(tri, p0) + _dot(tri, p1) + _dot(tri, p2)
            return f

        def operands(ci):
            def f():
                b_all = w.pop(("b", ci))
                for h in range(GLA_HEADS):
                    ks = slice(h * GLA_DK, (h + 1) * GLA_DK)
                    b = b_all[:, ks]
                    qh = q_ref[ci * C:(ci + 1) * C, ks]
                    kh = k_ref[ci * C:(ci + 1) * C, ks]
                    ends = [b[(i + 1) * GLA_SUB - 1:(i + 1) * GLA_SUB, :] for i in range(nb)]
                    starts = [None] + ends[:-1]
                    ref_start = jnp.concatenate([jnp.zeros((GLA_SUB, GLA_DK), F32)]
                                                + [rows16(e) for e in ends[:-1]], axis=0)
                    ref_end = jnp.concatenate([rows16(e) for e in ends], axis=0)
                    rel = b - ref_start
                    qd = qh * jnp.exp2(rel)
                    kd = kh * jnp.exp2(-rel)
                    ke = kh * jnp.exp2(ref_end - b)

                    def q_from(i, j):
                        if j == i - 1:
                            return blk(qd, i)
                        return blk(qd, i) * rows16(jnp.exp2(starts[i] - ends[j]))

                    def k_to(i, j):
                        if j == i:
                            return blk(ke, i)
                        return blk(ke, i) * rows16(jnp.exp2(ends[j] - ends[i]))

                    pairs = [(qd.astype(BF16), kd.astype(BF16))]
                    for (sz, _) in levels:
                        qblocks, kblocks = [], []
                        for i in range(nb):
                            mid = (i // (2 * sz)) * 2 * sz + sz
                            if (i // sz) % 2 == 1:
                                qblocks.append(q_from(i, mid - 1))
                                kblocks.append(None)
                            else:
                                qblocks.append(None)
                                kblocks.append(k_to(i, mid - 1))
                        pairs.append((stack(qblocks), stack(kblocks)))
                    q_in = stack([blk(qd, 0)]
                                 + [blk(qd, i) * rows16(jnp.exp2(starts[i])) for i in range(1, nb)])
                    k_dec = stack([k_to(i, nb - 1) for i in range(nb)])
                    w["ops", ci, h] = (q_in, k_dec, jnp.exp2(ends[nb - 1]), pairs)
            return f

        def attention(ci):
            def f():
                for h in range(GLA_HEADS):
                    q_in, k_dec, dec, pairs = w.pop(("ops", ci, h))
                    att = jnp.where(mask_diag, _dot_nt(*pairs[0]), 0.0)
                    for (_, same_blk), pr in zip(levels, pairs[1:]):
                        part = _dot_nt(*pr)
                        att = att + (part if same_blk is None else jnp.where(same_blk, part, 0.0))
                    w["att", ci, h] = (q_in, k_dec, dec, att.astype(BF16))
            return f

        def state_chain(ci):
            def f():
                bi = ci // chunks_per_seq
                for h in range(GLA_HEADS):
                    q_in, k_dec, dec, att = w.pop(("att", ci, h))
                    vs = slice(h * GLA_DV, (h + 1) * GLA_DV)
                    vh = v_ref[ci * C:(ci + 1) * C, vs]
                    st = st_ref[bi, h]
                    o_ref[ci * C:(ci + 1) * C, vs] = _dot(att, vh) + _dot(q_in, st.astype(BF16))
                    dec_col = jnp.broadcast_to(dec, (GLA_DK, GLA_DK)).T
                    dec_col = jnp.concatenate([dec_col] * (GLA_DV // GLA_DK), axis=1)
                    st_ref[bi, h] = st * dec_col + _dot_tn(k_dec, vh)
            return f

        def merge_c():
            parts = [_rms(o_ref[rows, h * GLA_DV:(h + 1) * GLA_DV], gla_g) for h in range(GLA_HEADS)]
            y_c = jnp.concatenate(parts, axis=-1) * _silu(z.pop("gc"))
            w["merged"] = w["merged"] + (_sigmoid(z.pop("gm2"))
                                         * _dot(y_c.astype(BF16), w_oc_ref[0]))

        def out_proj():
            w["out"] = _dot(w.pop("merged").astype(BF16), w_out_ref[0])

        def residual():
            out = w.pop("out")
            for idx, (bi, s0, n) in enumerate(segs):
                gate = mod_ref[0, bi, 2:3, :]
                y_ref[bi, s0:s0 + n, :] = (x_ref[bi, s0:s0 + n, :]
                                           + _rms(out[idx * n:(idx + 1) * n], post_g * gate))

        stages = [pool, merge_a, sgu_norm, sgu_mix, merge_b, log_decay]
        stages += [cumulative(ci) for ci in chunk_ids]
        stages += [operands(ci) for ci in chunk_ids]
        stages += [attention(ci) for ci in chunk_ids]
        stages += [state_chain(ci) for ci in chunk_ids]
        stages += [merge_c, out_proj, residual]
        return stages

    zs = [dict() for _ in range(NSUB)]
    for st_fn in projection_stages(0, zs[0]):
        st_fn()
    for sub in range(NSUB):
        nxt = projection_stages(sub + 1, zs[sub + 1]) if sub + 1 < NSUB else []
        _interleave(branch_stages(sub, zs[sub]), nxt)

    for bi in range(BB):
        tail = ext_ref[bi, TT:TT + HIST_ROWS, :]
        ext_ref[bi, 0:HIST_ROWS, :] = tail
        hist_ref[bi] = tail

    @pl.when(t == nt - 1)
    def _():
        sfin_ref[...] = st_ref[...]


def _layer_tiles(B, T):
    if T >= 512:
        BB, TT, NSUB = 1, 512, 2
    else:
        BB, TT, NSUB = min(B, 256 // T), T, 1
    L = min(T, SGU_LEN)
    C = 64
    R = BB * TT // NSUB
    assert T % TT == 0 and B % BB == 0 and TT >= HIST_ROWS
    assert (NSUB == 1 or BB == 1) and R % L == 0 and R % C == 0 and TT % C == 0
    return BB, TT, NSUB, L, C


def _vmem_limit(BB, TT, weights, has_state):
    M = BB * TT
    wbytes = sum(math.prod(w.shape[1:]) * w.dtype.itemsize for w in weights)
    io = 2 * 2 * M * D_MODEL * 4
    state = 2 * (2 if has_state else 1) * BB * (GLA_HEADS * GLA_DK * GLA_DV + HIST_ROWS * D_A) * 4
    scratch = M * (D_MODEL * 2 + 3 * D_CK * 4 + D_CV * 2 + D_CV * 4) + BB * (TT + HIST_ROWS) * D_A * 4 \
        + BB * GLA_HEADS * GLA_DK * GLA_DV * 4
    temps = M * (3 * D_MODEL + 6 * D_MODEL) * 4
    return min(wbytes + io + state + scratch + temps, V7X_VMEM_BYTES - 4 * 1024 * 1024)


def _mixer_layer(l, x, mod, mod_row0, state, weights, pos0, emit_vn):
    B, T, D = x.shape
    BB, TT, NSUB, L, C = _layer_tiles(B, T)
    M = BB * TT
    has_state = state is not None
    mod_blk0 = mod_row0 // BB

    def wspec(w):
        nd = w.ndim
        return pl.BlockSpec((1,) + w.shape[1:], lambda b, t: (l,) + (0,) * (nd - 1),
                            pipeline_mode=pl.Buffered(1))

    grid = (B // BB, T // TT)
    inputs = [x, mod]
    in_specs = [
        pl.BlockSpec((BB, TT, D), lambda b, t: (b, t, 0)),
        pl.BlockSpec((1, BB, 3, D), lambda b, t: (l, mod_blk0 + b, 0, 0)),
    ]
    if has_state:
        inputs += list(state)
        in_specs += [
            pl.BlockSpec((1, BB, HIST_ROWS, D_A), lambda b, t: (l, b, 0, 0)),
            pl.BlockSpec((1, BB, GLA_HEADS, GLA_DK, GLA_DV), lambda b, t: (l, b, 0, 0, 0)),
        ]
    inputs += list(weights)
    in_specs += [wspec(w) for w in weights]
    out_shape = [
        jax.ShapeDtypeStruct((B, T, D), F32),
        jax.ShapeDtypeStruct((B, HIST_ROWS, D_A), F32),
        jax.ShapeDtypeStruct((B, GLA_HEADS, GLA_DK, GLA_DV), F32),
    ]
    out_specs = [
        pl.BlockSpec((BB, TT, D), lambda b, t: (b, t, 0)),
        pl.BlockSpec((BB, HIST_ROWS, D_A), lambda b, t: (b, 0, 0)),
        pl.BlockSpec((BB, GLA_HEADS, GLA_DK, GLA_DV), lambda b, t: (b, 0, 0, 0)),
    ]
    if emit_vn:
        out_shape.append(jax.ShapeDtypeStruct((B, T, D_B), F32))
        out_specs.append(pl.BlockSpec((BB, TT, D_B), lambda b, t: (b, t, 0)))
    scratch_shapes = [
        pltpu.VMEM((M, D), BF16),
        pltpu.VMEM((BB, HIST_ROWS + TT, D_A), F32),
        pltpu.VMEM((M, D_CK), F32),
        pltpu.VMEM((M, D_CK), F32),
        pltpu.VMEM((M, D_CV), BF16),
        pltpu.VMEM((M, D_CK), F32),
        pltpu.VMEM((M, D_CV), F32),
        pltpu.VMEM((BB, GLA_HEADS, GLA_DK, GLA_DV), F32),
    ]
    body = functools.partial(_layer_kernel, BB=BB, TT=TT, NSUB=NSUB, L=L, C=C, pos0=pos0,
                             has_state=has_state, emit_vn=emit_vn)
    return pl.pallas_call(
        body,
        grid=grid,
        in_specs=in_specs,
        out_specs=out_specs,
        out_shape=out_shape,
        scratch_shapes=scratch_shapes,
        compiler_params=pltpu.CompilerParams(
            dimension_semantics=("arbitrary", "arbitrary"),
            vmem_limit_bytes=_vmem_limit(BB, TT, weights, has_state)),
        name="mixer_layer_T%d" % T,
    )(*inputs)


def kernel(x_prompt, x_sample, state_pool, state_gla, c_prompt, c_sample, ada_w, ada_b, pre_norm_g,
           post_norm_g, w_in, pool_w, pool_scale, sgu_norm_g, sgu_w, sgu_b, gla_wa2, gla_ba, gla_norm_g,
           w_oa, w_ob, w_oc, w_out):
    bp = x_prompt.shape[0]
    bs = x_sample.shape[0]
    mod = _ada_modulation(jnp.concatenate([c_prompt, c_sample], axis=0), ada_w, ada_b)
    mod = mod.reshape(DEPTH, bp + bs, 3, D_MODEL)

    w_pack = jnp.swapaxes(w_in, 1, 2).astype(BF16)
    sgu_bias = jnp.repeat(jnp.swapaxes(sgu_b, 1, 2), SGU_GW, axis=2)
    vec = lambda v: v[:, None, :]
    weights = [vec(pre_norm_g), vec(post_norm_g), w_pack, pool_w, vec(pool_scale), vec(sgu_norm_g),
               sgu_w, sgu_bias, gla_wa2, vec(gla_ba), vec(gla_norm_g),
               w_oa.astype(BF16), w_ob.astype(BF16), w_oc.astype(BF16), w_out.astype(BF16)]
    hist_sample = jnp.pad(state_pool, ((0, 0), (0, 0), (HIST_ROWS - POOL_HIST, 0), (0, 0)))

    xp, xs = x_prompt, x_sample
    pool_p, gla_p, pool_s, gla_s, sgu_s = [], [], [], [], []
    for l in range(DEPTH):
        xp, hp, sp = _mixer_layer(l, xp, mod, 0, None, weights, 0, False)
        xs, hs, ss, vs = _mixer_layer(l, xs, mod, bp, (hist_sample, state_gla), weights, PAST_LEN, True)
        pool_p.append(hp[:, HIST_ROWS - POOL_HIST:])
        gla_p.append(sp)
        pool_s.append(hs[:, HIST_ROWS - POOL_HIST:])
        gla_s.append(ss)
        sgu_s.append(vs)
    return (xp, xs, jnp.stack(pool_p), jnp.stack(gla_p), jnp.stack(pool_s), jnp.stack(gla_s),
            jnp.stack(sgu_s))
```

```python
import functools
import math

import jax
import jax.numpy as jnp
from jax import lax
from jax.experimental import pallas as pl
from jax.experimental.pallas import tpu as pltpu

D_MODEL = 1024
DEPTH = 2
PAST_LEN = 2048
EPS = 1e-6
POOL_WINDOWS = (2, 4, 8, 16)
POOL_GW = 128
D_A = 512
POOL_HIST = 15
HIST_ROWS = 16
SGU_LEN = 128
SGU_GROUPS = 4
D_B = 512
SGU_GW = 128
GLA_HEADS = 4
GLA_DK = 128
GLA_DV = 256
D_CK = 512
D_CV = 1024
GLA_RANK = 16
GLA_NORMALIZER = 16.0
GLA_SUB = 16
SUBTILE_SLOTS = 2
LANE = 128
V7X_VMEM_BYTES = 64 * 1024 * 1024
LOG2E = math.log2(math.e)

OFF_A, OFF_GA, OFF_U, OFF_VB, OFF_GB = 0, 512, 1024, 1536, 2048
OFF_Q, OFF_K, OFF_VC, OFF_GC = 2560, 3072, 3584, 4608
OFF_LR = 5632
OFF_GM = OFF_LR + GLA_RANK

F32 = jnp.float32
BF16 = jnp.bfloat16


def _dot(a, b):
    return jnp.dot(a, b, preferred_element_type=F32)


def _dot_nt(a, b):
    return lax.dot_general(a, b, (((1,), (1,)), ((), ())), preferred_element_type=F32)


def _dot_tn(a, b):
    return lax.dot_general(a, b, (((0,), (0,)), ((), ())), preferred_element_type=F32)


def _sigmoid(x):
    return 1.0 / (1.0 + jnp.exp2(x * (-LOG2E)))


def _silu(x):
    return x * _sigmoid(x)


def _rms(x, g):
    return x * lax.rsqrt(jnp.mean(x * x, axis=-1, keepdims=True) + EPS) * g


def _ada_kernel(c_ref, w_ref, b_ref, o_ref):
    c = c_ref[...]
    o_ref[0] = _dot(_silu(c).astype(BF16), w_ref[0].astype(BF16)) + b_ref[0]


def _ada_modulation(c_all, ada_w, ada_b):
    n = c_all.shape[0]
    nt = 3
    return pl.pallas_call(
        _ada_kernel,
        grid=(DEPTH, nt),
        in_specs=[
            pl.BlockSpec((n, D_MODEL), lambda l, j: (0, 0)),
            pl.BlockSpec((1, D_MODEL, D_MODEL), lambda l, j: (l, 0, j)),
            pl.BlockSpec((1, 1, D_MODEL), lambda l, j: (l, 0, j)),
        ],
        out_specs=pl.BlockSpec((1, n, D_MODEL), lambda l, j: (l, 0, j)),
        out_shape=jax.ShapeDtypeStruct((DEPTH, n, 3 * D_MODEL), F32),
        name="ada_modulation",
    )(c_all, ada_w, ada_b.reshape(DEPTH, 1, 3 * D_MODEL))


def _interleave(*stage_lists):
    lists = [sl for sl in stage_lists if sl]
    done = [0] * len(lists)
    while any(d < len(sl) for d, sl in zip(done, lists)):
        k = min((i for i in range(len(lists)) if done[i] < len(lists[i])),
                key=lambda i: done[i] / len(lists[i]))
        lists[k][done[k]]()
        done[k] += 1


def _layer_kernel(*refs, BB, TT, NSUB, L, C, pos0, has_state, emit_vn):
    refs = list(refs)
    x_ref, mod_ref = refs[:2]
    del refs[:2]
    if has_state:
        hist0_ref, s0_ref = refs[:2]
        del refs[:2]
    (pre_g_ref, post_g_ref, w_in_ref, pool_w_ref, pool_scale_ref, sgu_g_ref, sgu_w_ref, sgu_b_ref,
     wa2_ref, ba_ref, gla_g_ref, w_oa_ref, w_ob_ref, w_oc_ref, w_out_ref) = refs[:15]
    del refs[:15]
    y_ref, hist_ref, sfin_ref = refs[:3]
    del refs[:3]
    if emit_vn:
        vn_ref = refs.pop(0)
    hb_ref, ext_ref, q_ref, k_ref, v_ref, la_ref, o_ref, st_ref = refs

    M = BB * TT
    R = M // NSUB
    SLOTS = min(NSUB, SUBTILE_SLOTS)
    t = pl.program_id(1)
    nt = pl.num_programs(1)

    @pl.when(t == 0)
    def _():
        if has_state:
            ext_ref[:, 0:HIST_ROWS, :] = hist0_ref[0]
            st_ref[...] = s0_ref[0]
        else:
            ext_ref[:, 0:HIST_ROWS, :] = jnp.zeros((BB, HIST_ROWS, D_A), F32)
            st_ref[...] = jnp.zeros(st_ref.shape, F32)

    pre_g = pre_g_ref[0]
    post_g = post_g_ref[0]
    gla_g = gla_g_ref[0]
    ri = lax.broadcasted_iota(jnp.int32, (L, L), 0)
    ci_ = lax.broadcasted_iota(jnp.int32, (L, L), 1)
    wms = [jnp.where(ri >= ci_, sgu_w_ref[0, g, 0:L, 0:L], 0.0).astype(BF16) for g in range(SGU_GROUPS)]
    sgu_bias = sgu_b_ref[0, 0:L, :]
    wa2 = jnp.concatenate([wa2_ref[0], jnp.zeros((LANE - GLA_RANK, D_CK), F32)], axis=0).astype(BF16)
    cr = lax.broadcasted_iota(jnp.int32, (C, C), 0)
    cc = lax.broadcasted_iota(jnp.int32, (C, C), 1)
    tri = (cr >= cc).astype(BF16)
    sub_shift = GLA_SUB.bit_length() - 1
    mask_diag = (cr >= cc) & ((cr >> sub_shift) == (cc >> sub_shift))
    nb = C // GLA_SUB
    levels = []
    s = 1
    while 2 * s <= nb:
        sh = sub_shift + s.bit_length()
        levels.append((s, None if 2 * s == nb else (cr >> sh) == (cc >> sh)))
        s *= 2
    chunks_per_seq = TT // C

    def blk(x, i):
        return x[i * GLA_SUB:(i + 1) * GLA_SUB]

    def rows16(r):
        return jnp.broadcast_to(r, (GLA_SUB, r.shape[1]))

    def stack(blocks):
        zero = jnp.zeros((GLA_SUB, GLA_DK), F32)
        return jnp.concatenate([zero if b is None else b for b in blocks], axis=0).astype(BF16)

    def segments(sub):
        if BB == 1:
            return [(0, sub * R, R)]
        return [(bi, 0, TT) for bi in range(BB)]

    def projection_stages(sub, z):
        g0 = (sub % SLOTS) * R
        rows = slice(g0, g0 + R)

        def norm():
            for idx, (bi, s0, n) in enumerate(segments(sub)):
                xb = x_ref[bi, s0:s0 + n, :]
                shift = mod_ref[0, bi, 0:1, :]
                scale = mod_ref[0, bi, 1:2, :]
                hmod = _rms(xb, pre_g * (1.0 + scale)) + shift
                hb_ref[g0 + idx * n:g0 + (idx + 1) * n, :] = hmod.astype(BF16)

        def proj(off, width):
            return _dot_nt(hb_ref[rows, :], w_in_ref[0, off:off + width, :])

        def keep(name, off, width):
            def f():
                z[name] = proj(off, width)
            return f

        def proj_a():
            a = proj(OFF_A, D_A)
            z["a"] = a
            for idx, (bi, s0, n) in enumerate(segments(sub)):
                ext_ref[bi, HIST_ROWS + s0:HIST_ROWS + s0 + n, :] = a[idx * n:(idx + 1) * n]

        def proj_q():
            q_ref[rows, :] = proj(OFF_Q, D_CK) * (GLA_DK ** -0.5)

        def proj_k():
            k_ref[rows, :] = proj(OFF_K, D_CK)

        def proj_v():
            v_ref[rows, :] = proj(OFF_VC, D_CV).astype(BF16)

        early = [norm, proj_a, keep("ga", OFF_GA, D_A), keep("gm0", OFF_GM, D_MODEL),
                 keep("vb", OFF_VB, D_B), keep("u", OFF_U, D_B), keep("gb", OFF_GB, D_B),
                 keep("gm1", OFF_GM + D_MODEL, D_MODEL), keep("lr", OFF_LR, LANE), proj_q, proj_k]
        last = [keep("gc", OFF_GC, D_CV), keep("gm2", OFF_GM + 2 * D_MODEL, D_MODEL)]
        return early, [proj_v], last

    def branch_stages(sub, z):
        g0 = (sub % SLOTS) * R
        rows = slice(g0, g0 + R)
        segs = segments(sub)
        c_first = g0 // C
        chunk_ids = list(range(c_first, c_first + R // C))
        w = {}

        def pool():
            parts = []
            for idx, (bi, s0, n) in enumerate(segs):
                ab = z["a"][idx * n:(idx + 1) * n]
                row = lax.broadcasted_iota(jnp.int32, (n, POOL_GW), 0)
                pos1 = row + (t * TT + s0 + pos0 + 1)
                groups = []
                for gi, win in enumerate(POOL_WINDOWS):
                    ls = slice(gi * POOL_GW, (gi + 1) * POOL_GW)
                    base = HIST_ROWS + s0
                    terms = [ext_ref[bi, base - k:base - k + n, ls] for k in range(win)]
                    while len(terms) > 1:
                        terms = [terms[i] + terms[i + 1] for i in range(0, len(terms), 2)]
                    cnt = jnp.minimum(pos1, win).astype(F32)
                    dgi = terms[0] / cnt - ab[:, ls]
                    groups.append(_dot(dgi.astype(BF16), pool_w_ref[0, gi].astype(BF16)))
                parts.append(jnp.concatenate(groups, axis=-1))
            w["ya"] = parts[0] if len(parts) == 1 else jnp.concatenate(parts, axis=0)

        def merge_a():
            y_a = w.pop("ya") * pool_scale_ref[0] * _silu(z.pop("ga"))
            w["merged"] = (_sigmoid(z.pop("gm0"))
                           * _dot(y_a.astype(BF16), w_oa_ref[0]))

        def sgu_norm():
            vb = z.pop("vb")
            mu = jnp.mean(vb, axis=-1, keepdims=True)
            vc = vb - mu
            vn = vc * lax.rsqrt(jnp.mean(vc * vc, axis=-1, keepdims=True) + EPS) * sgu_g_ref[0]
            if emit_vn:
                for idx, (bi, s0, n) in enumerate(segs):
                    vn_ref[bi, s0:s0 + n, :] = vn[idx * n:(idx + 1) * n]
            w["vnb"] = vn.astype(BF16)

        def sgu_mix():
            vnb = w.pop("vnb")
            s_rows = []
            for c0 in range(0, R, L):
                parts = [_dot(wms[g], vnb[c0:c0 + L, g * SGU_GW:(g + 1) * SGU_GW])
                         for g in range(SGU_GROUPS)]
                s_rows.append(jnp.concatenate(parts, axis=-1) + sgu_bias)
            s_all = s_rows[0] if len(s_rows) == 1 else jnp.concatenate(s_rows, axis=0)
            w["yb"] = z.pop("u") * s_all * _silu(z.pop("gb"))

        def merge_b():
            w["merged"] = w["merged"] + (_sigmoid(z.pop("gm1"))
                                         * _dot(w.pop("yb").astype(BF16), w_ob_ref[0]))

        def log_decay():
            pre = _dot(z.pop("lr").astype(BF16), wa2) + ba_ref[0]
            log_sig = jnp.minimum(pre, 0.0) - jnp.log(1.0 + jnp.exp2(jnp.abs(pre) * (-LOG2E)))
            la_ref[rows, :] = log_sig * (LOG2E / GLA_NORMALIZER)

        def cumulative(ci):
            def f():
                la = la_ref[ci * C:(ci + 1) * C, :]
                p0 = la.astype(BF16)
                p1 = (la - p0.astype(F32)).astype(BF16)
                w["b", ci] = _dot(tri, p0) + _dot(tri, p1)
            return f

        def operands(ci):
            def f():
                b_all = w.pop(("b", ci))
                for h in range(GLA_HEADS):
                    ks = slice(h * GLA_DK, (h + 1) * GLA_DK)
                    b = b_all[:, ks]
                    qh = q_ref[ci * C:(ci + 1) * C, ks]
                    kh = k_ref[ci * C:(ci + 1) * C, ks]
                    ends = [b[(i + 1) * GLA_SUB - 1:(i + 1) * GLA_SUB, :] for i in range(nb)]
                    starts = [None] + ends[:-1]
                    ref_start = jnp.concatenate([jnp.zeros((GLA_SUB, GLA_DK), F32)]
                                                + [rows16(e) for e in ends[:-1]], axis=0)
                    ref_end = jnp.concatenate([rows16(e) for e in ends], axis=0)
                    rel = b - ref_start
                    qd = qh * jnp.exp2(rel)
                    kd = kh * jnp.exp2(-rel)
                    ke = kh * jnp.exp2(ref_end - b)

                    def q_from(i, j):
                        if j == i - 1:
                            return blk(qd, i)
                        return blk(qd, i) * rows16(jnp.exp2(starts[i] - ends[j]))

                    def k_to(i, j):
                        if j == i:
                            return blk(ke, i)
                        return blk(ke, i) * rows16(jnp.exp2(ends[j] - ends[i]))

                    pairs = [(qd.astype(BF16), kd.astype(BF16))]
                    for (sz, _) in levels:
                        qblocks, kblocks = [], []
                        for i in range(nb):
                            mid = (i // (2 * sz)) * 2 * sz + sz
                            if (i // sz) % 2 == 1:
                                qblocks.append(q_from(i, mid - 1))
                                kblocks.append(None)
                            else:
                                qblocks.append(None)
                                kblocks.append(k_to(i, mid - 1))
                        pairs.append((stack(qblocks), stack(kblocks)))
                    q_in = stack([blk(qd, 0)]
                                 + [blk(qd, i) * rows16(jnp.exp2(starts[i])) for i in range(1, nb)])
                    k_dec = stack([k_to(i, nb - 1) for i in range(nb)])
                    w["ops", ci, h] = (q_in, k_dec, jnp.exp2(ends[nb - 1]), pairs)
            return f

        def attention(ci):
            def f():
                for h in range(GLA_HEADS):
                    q_in, k_dec, dec, pairs = w.pop(("ops", ci, h))
                    att = jnp.where(mask_diag, _dot_nt(*pairs[0]), 0.0)
                    for (_, same_blk), pr in zip(levels, pairs[1:]):
                        part = _dot_nt(*pr)
                        att = att + (part if same_blk is None else jnp.where(same_blk, part, 0.0))
                    w["att", ci, h] = (q_in, k_dec, dec, att.astype(BF16))
            return f

        def state_chain(ci):
            def f():
                bi = 0 if BB == 1 else ci // chunks_per_seq
                for h in range(GLA_HEADS):
                    q_in, k_dec, dec, att = w.pop(("att", ci, h))
                    vs = slice(h * GLA_DV, (h + 1) * GLA_DV)
                    vh = v_ref[ci * C:(ci + 1) * C, vs]
                    st = st_ref[bi, h]
                    o_ref[ci * C:(ci + 1) * C, vs] = _dot(att, vh) + _dot(q_in, st.astype(BF16))
                    dec_col = jnp.broadcast_to(dec, (GLA_DK, GLA_DK)).T
                    dec_col = jnp.concatenate([dec_col] * (GLA_DV // GLA_DK), axis=1)
                    st_ref[bi, h] = st * dec_col + _dot_tn(k_dec, vh)
            return f

        def merge_c():
            parts = [_rms(o_ref[rows, h * GLA_DV:(h + 1) * GLA_DV], gla_g) for h in range(GLA_HEADS)]
            y_c = jnp.concatenate(parts, axis=-1) * _silu(z.pop("gc"))
            w["merged"] = w["merged"] + (_sigmoid(z.pop("gm2"))
                                         * _dot(y_c.astype(BF16), w_oc_ref[0]))

        def out_proj():
            w["out"] = _dot(w.pop("merged").astype(BF16), w_out_ref[0])

        def residual():
            out = w.pop("out")
            for idx, (bi, s0, n) in enumerate(segs):
                gate = mod_ref[0, bi, 2:3, :]
                y_ref[bi, s0:s0 + n, :] = (x_ref[bi, s0:s0 + n, :]
                                           + _rms(out[idx * n:(idx + 1) * n], post_g * gate))

        early = [pool, merge_a, sgu_norm, sgu_mix, merge_b, log_decay]
        early += [cumulative(ci) for ci in chunk_ids]
        early += [operands(ci) for ci in chunk_ids]
        late = [attention(ci) for ci in chunk_ids]
        late += [state_chain(ci) for ci in chunk_ids]
        late += [merge_c, out_proj, residual]
        return early, late

    zs = [dict() for _ in range(NSUB)]
    proj = [projection_stages(sub, zs[sub]) for sub in range(NSUB)]
    branch = [branch_stages(sub, zs[sub]) for sub in range(NSUB)]
    _interleave(proj[0][0])
    for r in range(NSUB + 1):
        matmul_side = ((proj[r - 1][2] if r >= 1 else []) + (proj[r][1] if r < NSUB else [])
                       + (proj[r + 1][0] if r + 1 < NSUB else []))
        _interleave(branch[r - 1][1] if r >= 1 else [], branch[r][0] if r < NSUB else [], matmul_side)

    for bi in range(BB):
        tail = ext_ref[bi, TT:TT + HIST_ROWS, :]
        ext_ref[bi, 0:HIST_ROWS, :] = tail
        hist_ref[bi] = tail

    @pl.when(t == nt - 1)
    def _():
        sfin_ref[...] = st_ref[...]


def _layer_tiles(B, T):
    if T >= 512:
        BB, TT, NSUB = 1, 512, 2
    else:
        BB, TT, NSUB = min(B, 256 // T), T, 1
    L = min(T, SGU_LEN)
    C = 64
    R = BB * TT // NSUB
    assert T % TT == 0 and B % BB == 0 and TT >= HIST_ROWS
    assert (NSUB == 1 or BB == 1) and R % L == 0 and R % C == 0 and TT % C == 0
    return BB, TT, NSUB, L, C


def _vmem_limit(BB, TT, NSUB, weights, has_state):
    M = BB * TT
    ring = min(NSUB, SUBTILE_SLOTS) * (M // NSUB)
    wbytes = sum(math.prod(w.shape[1:]) * w.dtype.itemsize for w in weights)
    io = 2 * 2 * M * D_MODEL * 4
    state = 2 * (2 if has_state else 1) * BB * (GLA_HEADS * GLA_DK * GLA_DV + HIST_ROWS * D_A) * 4
    scratch = ring * (D_MODEL * 2 + 3 * D_CK * 4 + D_CV * 2 + D_CV * 4) + BB * (TT + HIST_ROWS) * D_A * 4 \
        + BB * GLA_HEADS * GLA_DK * GLA_DV * 4
    temps = ring * (3 * D_MODEL + 6 * D_MODEL) * 4
    return min(wbytes + io + state + scratch + temps, V7X_VMEM_BYTES - 4 * 1024 * 1024)


def _mixer_layer(l, x, mod, mod_row0, state, weights, pos0, emit_vn):
    B, T, D = x.shape
    BB, TT, NSUB, L, C = _layer_tiles(B, T)
    ring = min(NSUB, SUBTILE_SLOTS) * (BB * TT // NSUB)
    has_state = state is not None
    mod_blk0 = mod_row0 // BB

    def wspec(w):
        nd = w.ndim
        return pl.BlockSpec((1,) + w.shape[1:], lambda b, t: (l,) + (0,) * (nd - 1),
                            pipeline_mode=pl.Buffered(1))

    grid = (B // BB, T // TT)
    inputs = [x, mod]
    in_specs = [
        pl.BlockSpec((BB, TT, D), lambda b, t: (b, t, 0)),
        pl.BlockSpec((1, BB, 3, D), lambda b, t: (l, mod_blk0 + b, 0, 0)),
    ]
    if has_state:
        inputs += list(state)
        in_specs += [
            pl.BlockSpec((1, BB, HIST_ROWS, D_A), lambda b, t: (l, b, 0, 0)),
            pl.BlockSpec((1, BB, GLA_HEADS, GLA_DK, GLA_DV), lambda b, t: (l, b, 0, 0, 0)),
        ]
    inputs += list(weights)
    in_specs += [wspec(w) for w in weights]
    out_shape = [
        jax.ShapeDtypeStruct((B, T, D), F32),
        jax.ShapeDtypeStruct((B, HIST_ROWS, D_A), F32),
        jax.ShapeDtypeStruct((B, GLA_HEADS, GLA_DK, GLA_DV), F32),
    ]
    out_specs = [
        pl.BlockSpec((BB, TT, D), lambda b, t: (b, t, 0)),
        pl.BlockSpec((BB, HIST_ROWS, D_A), lambda b, t: (b, 0, 0)),
        pl.BlockSpec((BB, GLA_HEADS, GLA_DK, GLA_DV), lambda b, t: (b, 0, 0, 0)),
    ]
    if emit_vn:
        out_shape.append(jax.ShapeDtypeStruct((B, T, D_B), F32))
        out_specs.append(pl.BlockSpec((BB, TT, D_B), lambda b, t: (b, t, 0)))
    scratch_shapes = [
        pltpu.VMEM((ring, D), BF16),
        pltpu.VMEM((BB, HIST_ROWS + TT, D_A), F32),
        pltpu.VMEM((ring, D_CK), F32),
        pltpu.VMEM((ring, D_CK), F32),
        pltpu.VMEM((ring, D_CV), BF16),
        pltpu.VMEM((ring, D_CK), F32),
        pltpu.VMEM((ring, D_CV), F32),
        pltpu.VMEM((BB, GLA_HEADS, GLA_DK, GLA_DV), F32),
    ]
    body = functools.partial(_layer_kernel, BB=BB, TT=TT, NSUB=NSUB, L=L, C=C, pos0=pos0,
                             has_state=has_state, emit_vn=emit_vn)
    return pl.pallas_call(
        body,
        grid=grid,
        in_specs=in_specs,
        out_specs=out_specs,
        out_shape=out_shape,
        scratch_shapes=scratch_shapes,
        compiler_params=pltpu.CompilerParams(
            dimension_semantics=("arbitrary", "arbitrary"),
            vmem_limit_bytes=_vmem_limit(BB, TT, NSUB, weights, has_state)),
        name="mixer_layer_T%d" % T,
    )(*inputs)


def kernel(x_prompt, x_sample, state_pool, state_gla, c_prompt, c_sample, ada_w, ada_b, pre_norm_g,
           post_norm_g, w_in, pool_w, pool_scale, sgu_norm_g, sgu_w, sgu_b, gla_wa2, gla_ba, gla_norm_g,
           w_oa, w_ob, w_oc, w_out):
    bp = x_prompt.shape[0]
    bs = x_sample.shape[0]
    mod = _ada_modulation(jnp.concatenate([c_prompt, c_sample], axis=0), ada_w, ada_b)
    mod = mod.reshape(DEPTH, bp + bs, 3, D_MODEL)

    w_pack = jnp.swapaxes(w_in, 1, 2).astype(BF16)
    sgu_bias = jnp.repeat(jnp.swapaxes(sgu_b, 1, 2), SGU_GW, axis=2)
    vec = lambda v: v[:, None, :]
    weights = [vec(pre_norm_g), vec(post_norm_g), w_pack, pool_w, vec(pool_scale), vec(sgu_norm_g),
               sgu_w, sgu_bias, gla_wa2, vec(gla_ba), vec(gla_norm_g),
               w_oa.astype(BF16), w_ob.astype(BF16), w_oc.astype(BF16), w_out.astype(BF16)]
    hist_sample = jnp.pad(state_pool, ((0, 0), (0, 0), (HIST_ROWS - POOL_HIST, 0), (0, 0)))

    xp, xs = x_prompt, x_sample
    pool_p, gla_p, pool_s, gla_s, sgu_s = [], [], [], [], []
    for l in range(DEPTH):
        xp, hp, sp = _mixer_layer(l, xp, mod, 0, None, weights, 0, False)
        xs, hs, ss, vs = _mixer_layer(l, xs, mod, bp, (hist_sample, state_gla), weights, PAST_LEN, True)
        pool_p.append(hp[:, HIST_ROWS - POOL_HIST:])
        gla_p.append(sp)
        pool_s.append(hs[:, HIST_ROWS - POOL_HIST:])
        gla_s.append(ss)
        sgu_s.append(vs)
    return (xp, xs, jnp.stack(pool_p), jnp.stack(gla_p), jnp.stack(pool_s), jnp.stack(gla_s),
            jnp.stack(sgu_s))
```

```python
import functools
import math

import jax
import jax.numpy as jnp
from jax import lax
from jax.experimental import pallas as pl
from jax.experimental.pallas import tpu as pltpu

D_MODEL = 1024
DEPTH = 2
PAST_LEN = 2048
EPS = 1e-6
POOL_WINDOWS = (2, 4, 8, 16)
POOL_GW = 128
D_A = 512
POOL_HIST = 15
HIST_ROWS = 16
SGU_LEN = 128
SGU_GROUPS = 4
D_B = 512
SGU_GW = 128
GLA_HEADS = 4
GLA_DK = 128
GLA_DV = 256
D_CK = 512
D_CV = 1024
GLA_RANK = 16
GLA_NORMALIZER = 16.0
GLA_SUB = 16
GLA_CHUNK = 64
SUBTILE_ROWS = 256
SUBTILES_PER_STEP = 2
SUBTILE_SLOTS = 2
ADA_COL_BLOCKS = 3
LANE = 128
V7X_VMEM_BYTES = 64 * 1024 * 1024
VMEM_RESERVE_BYTES = 4 * 1024 * 1024
LOG2E = math.log2(math.e)

OFF_A, OFF_GA, OFF_U, OFF_VB, OFF_GB = 0, 512, 1024, 1536, 2048
OFF_Q, OFF_K, OFF_VC, OFF_GC = 2560, 3072, 3584, 4608
OFF_LR = 5632
OFF_GM = OFF_LR + GLA_RANK

F32 = jnp.float32
BF16 = jnp.bfloat16


def _dot(a, b):
    return jnp.dot(a, b, preferred_element_type=F32)


def _dot_nt(a, b):
    return lax.dot_general(a, b, (((1,), (1,)), ((), ())), preferred_element_type=F32)


def _dot_tn(a, b):
    return lax.dot_general(a, b, (((0,), (0,)), ((), ())), preferred_element_type=F32)


def _sigmoid(x):
    return 1.0 / (1.0 + jnp.exp2(x * (-LOG2E)))


def _silu(x):
    return x * _sigmoid(x)


def _rms(x, g):
    return x * lax.rsqrt(jnp.mean(x * x, axis=-1, keepdims=True) + EPS) * g


def _ada_kernel(c_ref, w_ref, b_ref, o_ref):
    c = c_ref[...]
    o_ref[0] = _dot(_silu(c).astype(BF16), w_ref[0].astype(BF16)) + b_ref[0]


def _ada_modulation(c_all, ada_w, ada_b):
    n = c_all.shape[0]
    return pl.pallas_call(
        _ada_kernel,
        grid=(DEPTH, ADA_COL_BLOCKS),
        in_specs=[
            pl.BlockSpec((n, D_MODEL), lambda l, j: (0, 0)),
            pl.BlockSpec((1, D_MODEL, D_MODEL), lambda l, j: (l, 0, j)),
            pl.BlockSpec((1, 1, D_MODEL), lambda l, j: (l, 0, j)),
        ],
        out_specs=pl.BlockSpec((1, n, D_MODEL), lambda l, j: (l, 0, j)),
        out_shape=jax.ShapeDtypeStruct((DEPTH, n, 3 * D_MODEL), F32),
        name="ada_modulation",
    )(c_all, ada_w, ada_b.reshape(DEPTH, 1, 3 * D_MODEL))


def _interleave(*stage_lists):
    lists = [sl for sl in stage_lists if sl]
    done = [0] * len(lists)
    while any(d < len(sl) for d, sl in zip(done, lists)):
        k = min((i for i in range(len(lists)) if done[i] < len(lists[i])),
                key=lambda i: done[i] / len(lists[i]))
        lists[k][done[k]]()
        done[k] += 1


def _layer_kernel(*refs, BB, TT, NSUB, L, C, pos0, has_state, emit_vn):
    refs = list(refs)
    x_ref, mod_ref = refs[:2]
    del refs[:2]
    if has_state:
        hist0_ref, s0_ref = refs[:2]
        del refs[:2]
    (pre_g_ref, post_g_ref, w_in_ref, pool_w_ref, pool_scale_ref, sgu_g_ref, sgu_w_ref, sgu_b_ref,
     wa2_ref, ba_ref, gla_g_ref, w_oa_ref, w_ob_ref, w_oc_ref, w_out_ref) = refs[:15]
    del refs[:15]
    y_ref, hist_ref, sfin_ref = refs[:3]
    del refs[:3]
    if emit_vn:
        vn_ref = refs.pop(0)
    hb_ref, ext_ref, q_ref, k_ref, v_ref, la_ref, o_ref, st_ref = refs

    M = BB * TT
    R = M // NSUB
    SLOTS = min(NSUB, SUBTILE_SLOTS)
    t = pl.program_id(1)
    nt = pl.num_programs(1)

    @pl.when(t == 0)
    def _():
        if has_state:
            ext_ref[:, 0:HIST_ROWS, :] = hist0_ref[0]
            st_ref[...] = s0_ref[0]
        else:
            ext_ref[:, 0:HIST_ROWS, :] = jnp.zeros((BB, HIST_ROWS, D_A), F32)
            st_ref[...] = jnp.zeros(st_ref.shape, F32)

    pre_g = pre_g_ref[0]
    post_g = post_g_ref[0]
    gla_g = gla_g_ref[0]
    ri = lax.broadcasted_iota(jnp.int32, (L, L), 0)
    ci_ = lax.broadcasted_iota(jnp.int32, (L, L), 1)
    wms = [jnp.where(ri >= ci_, sgu_w_ref[0, g, 0:L, 0:L], 0.0).astype(BF16) for g in range(SGU_GROUPS)]
    sgu_bias = sgu_b_ref[0, 0:L, :]
    wa2 = jnp.concatenate([wa2_ref[0], jnp.zeros((LANE - GLA_RANK, D_CK), F32)], axis=0).astype(BF16)
    cr = lax.broadcasted_iota(jnp.int32, (C, C), 0)
    cc = lax.broadcasted_iota(jnp.int32, (C, C), 1)
    tri = (cr >= cc).astype(BF16)
    sub_shift = GLA_SUB.bit_length() - 1
    mask_diag = (cr >= cc) & ((cr >> sub_shift) == (cc >> sub_shift))
    nb = C // GLA_SUB
    levels = []
    s = 1
    while 2 * s <= nb:
        sh = sub_shift + s.bit_length()
        levels.append((s, None if 2 * s == nb else (cr >> sh) == (cc >> sh)))
        s *= 2
    chunks_per_seq = TT // C

    def blk(x, i):
        return x[i * GLA_SUB:(i + 1) * GLA_SUB]

    def rows16(r):
        return jnp.broadcast_to(r, (GLA_SUB, r.shape[1]))

    def stack(blocks):
        zero = jnp.zeros((GLA_SUB, GLA_DK), F32)
        return jnp.concatenate([zero if b is None else b for b in blocks], axis=0).astype(BF16)

    def segments(sub):
        if BB == 1:
            return [(0, sub * R, R)]
        return [(bi, 0, TT) for bi in range(BB)]

    def projection_stages(sub, z):
        g0 = (sub % SLOTS) * R
        rows = slice(g0, g0 + R)

        def norm():
            for idx, (bi, s0, n) in enumerate(segments(sub)):
                xb = x_ref[bi, s0:s0 + n, :]
                shift = mod_ref[0, bi, 0:1, :]
                scale = mod_ref[0, bi, 1:2, :]
                hmod = _rms(xb, pre_g * (1.0 + scale)) + shift
                hb_ref[g0 + idx * n:g0 + (idx + 1) * n, :] = hmod.astype(BF16)

        def proj(off, width):
            return _dot_nt(hb_ref[rows, :], w_in_ref[0, off:off + width, :])

        def keep(name, off, width):
            def f():
                z[name] = proj(off, width)
            return f

        def proj_a():
            a = proj(OFF_A, D_A)
            z["a"] = a
            for idx, (bi, s0, n) in enumerate(segments(sub)):
                ext_ref[bi, HIST_ROWS + s0:HIST_ROWS + s0 + n, :] = a[idx * n:(idx + 1) * n]

        def proj_q():
            q_ref[rows, :] = proj(OFF_Q, D_CK) * (GLA_DK ** -0.5)

        def proj_k():
            k_ref[rows, :] = proj(OFF_K, D_CK)

        def proj_v():
            v_ref[rows, :] = proj(OFF_VC, D_CV).astype(BF16)

        early = [norm, proj_a, keep("ga", OFF_GA, D_A), keep("gm0", OFF_GM, D_MODEL),
                 keep("vb", OFF_VB, D_B), keep("u", OFF_U, D_B), keep("gb", OFF_GB, D_B),
                 keep("gm1", OFF_GM + D_MODEL, D_MODEL), keep("lr", OFF_LR, LANE), proj_q, proj_k]
        last = [keep("gc", OFF_GC, D_CV), keep("gm2", OFF_GM + 2 * D_MODEL, D_MODEL)]
        return early, [proj_v], last

    def branch_stages(sub, z):
        g0 = (sub % SLOTS) * R
        rows = slice(g0, g0 + R)
        segs = segments(sub)
        c_first = g0 // C
        chunk_ids = list(range(c_first, c_first + R // C))
        w = {}

        def pool():
            parts = []
            for idx, (bi, s0, n) in enumerate(segs):
                ab = z["a"][idx * n:(idx + 1) * n]
                row = lax.broadcasted_iota(jnp.int32, (n, POOL_GW), 0)
                pos1 = row + (t * TT + s0 + pos0 + 1)
                groups = []
                for gi, win in enumerate(POOL_WINDOWS):
                    ls = slice(gi * POOL_GW, (gi + 1) * POOL_GW)
                    base = HIST_ROWS + s0
                    terms = [ext_ref[bi, base - k:base - k + n, ls] for k in range(win)]
                    while len(terms) > 1:
                        terms = [terms[i] + terms[i + 1] for i in range(0, len(terms), 2)]
                    cnt = jnp.minimum(pos1, win).astype(F32)
                    dgi = terms[0] / cnt - ab[:, ls]
                    groups.append(_dot(dgi.astype(BF16), pool_w_ref[0, gi].astype(BF16)))
                parts.append(jnp.concatenate(groups, axis=-1))
            w["ya"] = parts[0] if len(parts) == 1 else jnp.concatenate(parts, axis=0)

        def merge_a():
            y_a = w.pop("ya") * pool_scale_ref[0] * _silu(z.pop("ga"))
            w["merged"] = (_sigmoid(z.pop("gm0"))
                           * _dot(y_a.astype(BF16), w_oa_ref[0]))

        def sgu_norm():
            vb = z.pop("vb")
            mu = jnp.mean(vb, axis=-1, keepdims=True)
            vc = vb - mu
            vn = vc * lax.rsqrt(jnp.mean(vc * vc, axis=-1, keepdims=True) + EPS) * sgu_g_ref[0]
            if emit_vn:
                for idx, (bi, s0, n) in enumerate(segs):
                    vn_ref[bi, s0:s0 + n, :] = vn[idx * n:(idx + 1) * n]
            w["vnb"] = vn.astype(BF16)

        def sgu_mix():
            vnb = w.pop("vnb")
            s_rows = []
            for c0 in range(0, R, L):
                parts = [_dot(wms[g], vnb[c0:c0 + L, g * SGU_GW:(g + 1) * SGU_GW])
                         for g in range(SGU_GROUPS)]
                s_rows.append(jnp.concatenate(parts, axis=-1) + sgu_bias)
            s_all = s_rows[0] if len(s_rows) == 1 else jnp.concatenate(s_rows, axis=0)
            w["yb"] = z.pop("u") * s_all * _silu(z.pop("gb"))

        def merge_b():
            w["merged"] = w["merged"] + (_sigmoid(z.pop("gm1"))
                                         * _dot(w.pop("yb").astype(BF16), w_ob_ref[0]))

        def log_decay():
            pre = _dot(z.pop("lr").astype(BF16), wa2) + ba_ref[0]
            log_sig = jnp.minimum(pre, 0.0) - jnp.log(1.0 + jnp.exp2(jnp.abs(pre) * (-LOG2E)))
            la_ref[rows, :] = log_sig * (LOG2E / GLA_NORMALIZER)

        def cumulative(ci):
            def f():
                la = la_ref[ci * C:(ci + 1) * C, :]
                p0 = la.astype(BF16)
                p1 = (la - p0.astype(F32)).astype(BF16)
                w["b", ci] = _dot(tri, p0) + _dot(tri, p1)
            return f

        def operands(ci):
            def f():
                b_all = w.pop(("b", ci))
                for h in range(GLA_HEADS):
                    ks = slice(h * GLA_DK, (h + 1) * GLA_DK)
                    b = b_all[:, ks]
                    qh = q_ref[ci * C:(ci + 1) * C, ks]
                    kh = k_ref[ci * C:(ci + 1) * C, ks]
                    ends = [b[(i + 1) * GLA_SUB - 1:(i + 1) * GLA_SUB, :] for i in range(nb)]
                    starts = [None] + ends[:-1]
                    ref_start = jnp.concatenate([jnp.zeros((GLA_SUB, GLA_DK), F32)]
                                                + [rows16(e) for e in ends[:-1]], axis=0)
                    ref_end = jnp.concatenate([rows16(e) for e in ends], axis=0)
                    rel = b - ref_start
                    qd = qh * jnp.exp2(rel)
                    kd = kh * jnp.exp2(-rel)
                    ke = kh * jnp.exp2(ref_end - b)

                    def q_from(i, j):
                        if j == i - 1:
                            return blk(qd, i)
                        return blk(qd, i) * rows16(jnp.exp2(starts[i] - ends[j]))

                    def k_to(i, j):
                        if j == i:
                            return blk(ke, i)
                        return blk(ke, i) * rows16(jnp.exp2(ends[j] - ends[i]))

                    pairs = [(qd.astype(BF16), kd.astype(BF16))]
                    for (sz, _) in levels:
                        qblocks, kblocks = [], []
                        for i in range(nb):
                            mid = (i // (2 * sz)) * 2 * sz + sz
                            if (i // sz) % 2 == 1:
                                qblocks.append(q_from(i, mid - 1))
                                kblocks.append(None)
                            else:
                                qblocks.append(None)
                                kblocks.append(k_to(i, mid - 1))
                        pairs.append((stack(qblocks), stack(kblocks)))
                    q_in = stack([blk(qd, 0)]
                                 + [blk(qd, i) * rows16(jnp.exp2(starts[i])) for i in range(1, nb)])
                    k_dec = stack([k_to(i, nb - 1) for i in range(nb)])
                    w["ops", ci, h] = (q_in, k_dec, jnp.exp2(ends[nb - 1]), pairs)
            return f

        def attention(ci):
            def f():
                for h in range(GLA_HEADS):
                    q_in, k_dec, dec, pairs = w.pop(("ops", ci, h))
                    att = jnp.where(mask_diag, _dot_nt(*pairs[0]), 0.0)
                    for (_, same_blk), pr in zip(levels, pairs[1:]):
                        part = _dot_nt(*pr)
                        att = att + (part if same_blk is None else jnp.where(same_blk, part, 0.0))
                    w["att", ci, h] = (q_in, k_dec, dec, att.astype(BF16))
            return f

        def state_chain(ci):
            def f():
                bi = 0 if BB == 1 else ci // chunks_per_seq
                for h in range(GLA_HEADS):
                    q_in, k_dec, dec, att = w.pop(("att", ci, h))
                    vs = slice(h * GLA_DV, (h + 1) * GLA_DV)
                    vh = v_ref[ci * C:(ci + 1) * C, vs]
                    st = st_ref[bi, h]
                    o_ref[ci * C:(ci + 1) * C, vs] = _dot(att, vh) + _dot(q_in, st.astype(BF16))
                    dec_col = jnp.broadcast_to(dec, (GLA_DK, GLA_DK)).T
                    dec_col = jnp.concatenate([dec_col] * (GLA_DV // GLA_DK), axis=1)
                    st_ref[bi, h] = st * dec_col + _dot_tn(k_dec, vh)
            return f

        def merge_c():
            parts = [_rms(o_ref[rows, h * GLA_DV:(h + 1) * GLA_DV], gla_g) for h in range(GLA_HEADS)]
            y_c = jnp.concatenate(parts, axis=-1) * _silu(z.pop("gc"))
            w["merged"] = w["merged"] + (_sigmoid(z.pop("gm2"))
                                         * _dot(y_c.astype(BF16), w_oc_ref[0]))

        def out_proj():
            w["out"] = _dot(w.pop("merged").astype(BF16), w_out_ref[0])

        def residual():
            out = w.pop("out")
            for idx, (bi, s0, n) in enumerate(segs):
                gate = mod_ref[0, bi, 2:3, :]
                y_ref[bi, s0:s0 + n, :] = (x_ref[bi, s0:s0 + n, :]
                                           + _rms(out[idx * n:(idx + 1) * n], post_g * gate))

        early = [pool, merge_a, sgu_norm, sgu_mix, merge_b, log_decay]
        early += [cumulative(ci) for ci in chunk_ids]
        early += [operands(ci) for ci in chunk_ids]
        late = [attention(ci) for ci in chunk_ids]
        late += [state_chain(ci) for ci in chunk_ids]
        late += [merge_c, out_proj, residual]
        return early, late

    zs = [dict() for _ in range(NSUB)]
    proj = [projection_stages(sub, zs[sub]) for sub in range(NSUB)]
    branch = [branch_stages(sub, zs[sub]) for sub in range(NSUB)]
    _interleave(proj[0][0])
    for r in range(NSUB + 1):
        matmul_side = ((proj[r - 1][2] if r >= 1 else []) + (proj[r][1] if r < NSUB else [])
                       + (proj[r + 1][0] if r + 1 < NSUB else []))
        _interleave(branch[r - 1][1] if r >= 1 else [], branch[r][0] if r < NSUB else [], matmul_side)

    for bi in range(BB):
        tail = ext_ref[bi, TT:TT + HIST_ROWS, :]
        ext_ref[bi, 0:HIST_ROWS, :] = tail
        hist_ref[bi] = tail

    @pl.when(t == nt - 1)
    def _():
        sfin_ref[...] = st_ref[...]


def _layer_tiles(B, T):
    if T >= SUBTILES_PER_STEP * SUBTILE_ROWS:
        BB, TT, NSUB = 1, SUBTILES_PER_STEP * SUBTILE_ROWS, SUBTILES_PER_STEP
    else:
        BB, TT, NSUB = min(B, SUBTILE_ROWS // T), T, 1
    L = min(T, SGU_LEN)
    C = GLA_CHUNK
    R = BB * TT // NSUB
    assert T % TT == 0 and B % BB == 0 and TT >= HIST_ROWS
    assert (NSUB == 1 or BB == 1) and R % L == 0 and R % C == 0 and TT % C == 0
    return BB, TT, NSUB, L, C


def _vmem_limit(BB, TT, NSUB, weights, has_state):
    M = BB * TT
    ring = min(NSUB, SUBTILE_SLOTS) * (M // NSUB)
    wbytes = sum(math.prod(w.shape[1:]) * w.dtype.itemsize for w in weights)
    io = 2 * 2 * M * D_MODEL * 4
    state = 2 * (2 if has_state else 1) * BB * (GLA_HEADS * GLA_DK * GLA_DV + HIST_ROWS * D_A) * 4
    scratch = ring * (D_MODEL * 2 + 3 * D_CK * 4 + D_CV * 2 + D_CV * 4) + BB * (TT + HIST_ROWS) * D_A * 4 \
        + BB * GLA_HEADS * GLA_DK * GLA_DV * 4
    temps = ring * (3 * D_MODEL + 6 * D_MODEL) * 4
    return min(wbytes + io + state + scratch + temps, V7X_VMEM_BYTES - VMEM_RESERVE_BYTES)


def _mixer_layer(l, x, mod, mod_row0, state, weights, pos0, emit_vn):
    B, T, D = x.shape
    BB, TT, NSUB, L, C = _layer_tiles(B, T)
    ring = min(NSUB, SUBTILE_SLOTS) * (BB * TT // NSUB)
    has_state = state is not None
    mod_blk0 = mod_row0 // BB

    def wspec(w):
        nd = w.ndim
        return pl.BlockSpec((1,) + w.shape[1:], lambda b, t: (l,) + (0,) * (nd - 1),
                            pipeline_mode=pl.Buffered(1))

    grid = (B // BB, T // TT)
    inputs = [x, mod]
    in_specs = [
        pl.BlockSpec((BB, TT, D), lambda b, t: (b, t, 0)),
        pl.BlockSpec((1, BB, 3, D), lambda b, t: (l, mod_blk0 + b, 0, 0)),
    ]
    if has_state:
        inputs += list(state)
        in_specs += [
            pl.BlockSpec((1, BB, HIST_ROWS, D_A), lambda b, t: (l, b, 0, 0)),
            pl.BlockSpec((1, BB, GLA_HEADS, GLA_DK, GLA_DV), lambda b, t: (l, b, 0, 0, 0)),
        ]
    inputs += list(weights)
    in_specs += [wspec(w) for w in weights]
    out_shape = [
        jax.ShapeDtypeStruct((B, T, D), F32),
        jax.ShapeDtypeStruct((B, HIST_ROWS, D_A), F32),
        jax.ShapeDtypeStruct((B, GLA_HEADS, GLA_DK, GLA_DV), F32),
    ]
    out_specs = [
        pl.BlockSpec((BB, TT, D), lambda b, t: (b, t, 0)),
        pl.BlockSpec((BB, HIST_ROWS, D_A), lambda b, t: (b, 0, 0)),
        pl.BlockSpec((BB, GLA_HEADS, GLA_DK, GLA_DV), lambda b, t: (b, 0, 0, 0)),
    ]
    if emit_vn:
        out_shape.append(jax.ShapeDtypeStruct((B, T, D_B), F32))
        out_specs.append(pl.BlockSpec((BB, TT, D_B), lambda b, t: (b, t, 0)))
    scratch_shapes = [
        pltpu.VMEM((ring, D), BF16),
        pltpu.VMEM((BB, HIST_ROWS + TT, D_A), F32),
        pltpu.VMEM((ring, D_CK), F32),
        pltpu.VMEM((ring, D_CK), F32),
        pltpu.VMEM((ring, D_CV), BF16),
        pltpu.VMEM((ring, D_CK), F32),
        pltpu.VMEM((ring, D_CV), F32),
        pltpu.VMEM((BB, GLA_HEADS, GLA_DK, GLA_DV), F32),
    ]
    body = functools.partial(_layer_kernel, BB=BB, TT=TT, NSUB=NSUB, L=L, C=C, pos0=pos0,
                             has_state=has_state, emit_vn=emit_vn)
    return pl.pallas_call(
        body,
        grid=grid,
        in_specs=in_specs,
        out_specs=out_specs,
        out_shape=out_shape,
        scratch_shapes=scratch_shapes,
        compiler_params=pltpu.CompilerParams(
            dimension_semantics=("arbitrary", "arbitrary"),
            vmem_limit_bytes=_vmem_limit(BB, TT, NSUB, weights, has_state)),
        name="mixer_layer_T%d" % T,
    )(*inputs)


def kernel(x_prompt, x_sample, state_pool, state_gla, c_prompt, c_sample, ada_w, ada_b, pre_norm_g,
           post_norm_g, w_in, pool_w, pool_scale, sgu_norm_g, sgu_w, sgu_b, gla_wa2, gla_ba, gla_norm_g,
           w_oa, w_ob, w_oc, w_out):
    bp = x_prompt.shape[0]
    bs = x_sample.shape[0]
    mod = _ada_modulation(jnp.concatenate([c_prompt, c_sample], axis=0), ada_w, ada_b)
    mod = mod.reshape(DEPTH, bp + bs, 3, D_MODEL)

    w_pack = jnp.swapaxes(w_in, 1, 2).astype(BF16)
    sgu_bias = jnp.repeat(jnp.swapaxes(sgu_b, 1, 2), SGU_GW, axis=2)
    vec = lambda v: v[:, None, :]
    weights = [vec(pre_norm_g), vec(post_norm_g), w_pack, pool_w, vec(pool_scale), vec(sgu_norm_g),
               sgu_w, sgu_bias, gla_wa2, vec(gla_ba), vec(gla_norm_g),
               w_oa.astype(BF16), w_ob.astype(BF16), w_oc.astype(BF16), w_out.astype(BF16)]
    hist_sample = jnp.pad(state_pool, ((0, 0), (0, 0), (HIST_ROWS - POOL_HIST, 0), (0, 0)))

    xp, xs = x_prompt, x_sample
    pool_p, gla_p, pool_s, gla_s, sgu_s = [], [], [], [], []
    for l in range(DEPTH):
        xp, hp, sp = _mixer_layer(l, xp, mod, 0, None, weights, 0, False)
        xs, hs, ss, vs = _mixer_layer(l, xs, mod, bp, (hist_sample, state_gla), weights, PAST_LEN, True)
        pool_p.append(hp[:, HIST_ROWS - POOL_HIST:])
        gla_p.append(sp)
        pool_s.append(hs[:, HIST_ROWS - POOL_HIST:])
        gla_s.append(ss)
        sgu_s.append(vs)
    return (xp, xs, jnp.stack(pool_p), jnp.stack(gla_p), jnp.stack(pool_s), jnp.stack(gla_s),
            jnp.stack(sgu_s))
```

```python
import functools
import math

import jax
import jax.numpy as jnp
from jax import lax
from jax.experimental import pallas as pl
from jax.experimental.pallas import tpu as pltpu

D_MODEL = 1024
DEPTH = 2
PAST_LEN = 2048
EPS = 1e-6
POOL_WINDOWS = (2, 4, 8, 16)
POOL_GW = 128
D_A = 512
POOL_HIST = 15
HIST_ROWS = 16
SGU_LEN = 128
SGU_GROUPS = 4
D_B = 512
SGU_GW = 128
GLA_HEADS = 4
GLA_DK = 128
GLA_DV = 256
D_CK = 512
D_CV = 1024
GLA_RANK = 16
GLA_NORMALIZER = 16.0
GLA_SUB = 16
GLA_CHUNK = 64
SUBTILE_ROWS = 256
SUBTILES_PER_STEP = 2
SUBTILE_SLOTS = 2
ADA_COL_BLOCKS = 3
LANE = 128
V7X_VMEM_BYTES = 64 * 1024 * 1024
VMEM_RESERVE_BYTES = 4 * 1024 * 1024
LOG2E = math.log2(math.e)

OFF_A, OFF_GA, OFF_U, OFF_VB, OFF_GB = 0, 512, 1024, 1536, 2048
OFF_Q, OFF_K, OFF_VC, OFF_GC = 2560, 3072, 3584, 4608
OFF_LR = 5632
OFF_GM = OFF_LR + GLA_RANK

F32 = jnp.float32
BF16 = jnp.bfloat16


def _dot(a, b):
    return jnp.dot(a, b, preferred_element_type=F32)


def _dot_nt(a, b):
    return lax.dot_general(a, b, (((1,), (1,)), ((), ())), preferred_element_type=F32)


def _dot_tn(a, b):
    return lax.dot_general(a, b, (((0,), (0,)), ((), ())), preferred_element_type=F32)


def _sigmoid(x):
    return 1.0 / (1.0 + jnp.exp2(x * (-LOG2E)))


def _silu(x):
    return x * _sigmoid(x)


def _rms(x, g):
    return x * lax.rsqrt(jnp.mean(x * x, axis=-1, keepdims=True) + EPS) * g


def _ada_kernel(c_ref, w_ref, b_ref, o_ref):
    c = c_ref[...]
    o_ref[0] = _dot(_silu(c).astype(BF16), w_ref[0].astype(BF16)) + b_ref[0]


def _ada_modulation(c_all, ada_w, ada_b):
    n = c_all.shape[0]
    return pl.pallas_call(
        _ada_kernel,
        grid=(DEPTH, ADA_COL_BLOCKS),
        in_specs=[
            pl.BlockSpec((n, D_MODEL), lambda l, j: (0, 0)),
            pl.BlockSpec((1, D_MODEL, D_MODEL), lambda l, j: (l, 0, j)),
            pl.BlockSpec((1, 1, D_MODEL), lambda l, j: (l, 0, j)),
        ],
        out_specs=pl.BlockSpec((1, n, D_MODEL), lambda l, j: (l, 0, j)),
        out_shape=jax.ShapeDtypeStruct((DEPTH, n, 3 * D_MODEL), F32),
        name="ada_modulation",
    )(c_all, ada_w, ada_b.reshape(DEPTH, 1, 3 * D_MODEL))


def _weighted(fn, cost, ready=None):
    fn.cost = cost
    fn.ready = ready
    return fn


def _interleave(*stage_lists):
    lists = [sl for sl in stage_lists if sl]
    total = [sum(getattr(f, "cost", 1.0) for f in sl) for sl in lists]
    done = [0] * len(lists)
    spent = [0.0] * len(lists)
    while any(d < len(sl) for d, sl in zip(done, lists)):
        order = sorted((i for i in range(len(lists)) if done[i] < len(lists[i])),
                       key=lambda i: spent[i] / total[i])
        k = next(i for i in order
                 if getattr(lists[i][done[i]], "ready", None) is None or lists[i][done[i]].ready())
        fn = lists[k][done[k]]
        fn()
        done[k] += 1
        spent[k] += getattr(fn, "cost", 1.0)


def _layer_kernel(*refs, BB, TT, NSUB, L, C, pos0, has_state, emit_vn):
    refs = list(refs)
    x_ref, mod_ref = refs[:2]
    del refs[:2]
    if has_state:
        hist0_ref, s0_ref = refs[:2]
        del refs[:2]
    (pre_g_ref, post_g_ref, w_in_ref, pool_w_ref, pool_scale_ref, sgu_g_ref, sgu_w_ref, sgu_b_ref,
     wa2_ref, ba_ref, gla_g_ref, w_oa_ref, w_ob_ref, w_oc_ref, w_out_ref) = refs[:15]
    del refs[:15]
    y_ref, hist_ref, sfin_ref = refs[:3]
    del refs[:3]
    if emit_vn:
        vn_ref = refs.pop(0)
    hb_ref, ext_ref, q_ref, k_ref, v_ref, la_ref, o_ref, st_ref = refs

    M = BB * TT
    R = M // NSUB
    SLOTS = min(NSUB, SUBTILE_SLOTS)
    t = pl.program_id(1)
    nt = pl.num_programs(1)

    @pl.when(t == 0)
    def _():
        if has_state:
            ext_ref[:, 0:HIST_ROWS, :] = hist0_ref[0]
            st_ref[...] = s0_ref[0]
        else:
            ext_ref[:, 0:HIST_ROWS, :] = jnp.zeros((BB, HIST_ROWS, D_A), F32)
            st_ref[...] = jnp.zeros(st_ref.shape, F32)

    pre_g = pre_g_ref[0]
    post_g = post_g_ref[0]
    gla_g = gla_g_ref[0]
    ri = lax.broadcasted_iota(jnp.int32, (L, L), 0)
    ci_ = lax.broadcasted_iota(jnp.int32, (L, L), 1)
    wms = [jnp.where(ri >= ci_, sgu_w_ref[0, g, 0:L, 0:L], 0.0).astype(BF16) for g in range(SGU_GROUPS)]
    sgu_bias = sgu_b_ref[0, 0:L, :]
    wa2 = jnp.concatenate([wa2_ref[0], jnp.zeros((LANE - GLA_RANK, D_CK), F32)], axis=0).astype(BF16)
    cr = lax.broadcasted_iota(jnp.int32, (C, C), 0)
    cc = lax.broadcasted_iota(jnp.int32, (C, C), 1)
    tri = (cr >= cc).astype(BF16)
    sub_shift = GLA_SUB.bit_length() - 1
    mask_diag = (cr >= cc) & ((cr >> sub_shift) == (cc >> sub_shift))
    nb = C // GLA_SUB
    levels = []
    s = 1
    while 2 * s <= nb:
        sh = sub_shift + s.bit_length()
        levels.append((s, None if 2 * s == nb else (cr >> sh) == (cc >> sh)))
        s *= 2
    chunks_per_seq = TT // C

    def blk(x, i):
        return x[i * GLA_SUB:(i + 1) * GLA_SUB]

    def rows16(r):
        return jnp.broadcast_to(r, (GLA_SUB, r.shape[1]))

    def stack(blocks):
        zero = jnp.zeros((GLA_SUB, GLA_DK), F32)
        return jnp.concatenate([zero if b is None else b for b in blocks], axis=0).astype(BF16)

    def segments(sub):
        if BB == 1:
            return [(0, sub * R, R)]
        return [(bi, 0, TT) for bi in range(BB)]

    def projection_stages(sub, z):
        g0 = (sub % SLOTS) * R
        rows = slice(g0, g0 + R)

        def norm():
            for idx, (bi, s0, n) in enumerate(segments(sub)):
                xb = x_ref[bi, s0:s0 + n, :]
                shift = mod_ref[0, bi, 0:1, :]
                scale = mod_ref[0, bi, 1:2, :]
                hmod = _rms(xb, pre_g * (1.0 + scale)) + shift
                hb_ref[g0 + idx * n:g0 + (idx + 1) * n, :] = hmod.astype(BF16)

        def proj(off, width):
            return _dot_nt(hb_ref[rows, :], w_in_ref[0, off:off + width, :])

        def keep(name, off, width):
            def f():
                z[name] = proj(off, width)
            return f

        def proj_a():
            a = proj(OFF_A, D_A)
            z["a"] = a
            for idx, (bi, s0, n) in enumerate(segments(sub)):
                ext_ref[bi, HIST_ROWS + s0:HIST_ROWS + s0 + n, :] = a[idx * n:(idx + 1) * n]

        def proj_q():
            q_ref[rows, :] = proj(OFF_Q, D_CK) * (GLA_DK ** -0.5)

        def proj_k():
            k_ref[rows, :] = proj(OFF_K, D_CK)

        def proj_v():
            v_ref[rows, :] = proj(OFF_VC, D_CV).astype(BF16)

        early = [_weighted(norm, 1), _weighted(proj_a, 2), _weighted(keep("ga", OFF_GA, D_A), 2),
                 _weighted(keep("gm0", OFF_GM, D_MODEL), 4), _weighted(keep("vb", OFF_VB, D_B), 2),
                 _weighted(keep("u", OFF_U, D_B), 2), _weighted(keep("gb", OFF_GB, D_B), 2),
                 _weighted(keep("gm1", OFF_GM + D_MODEL, D_MODEL), 4), _weighted(keep("lr", OFF_LR, LANE), 1),
                 _weighted(proj_q, 2), _weighted(proj_k, 2)]
        last = [_weighted(keep("gc", OFF_GC, D_CV), 4),
                _weighted(keep("gm2", OFF_GM + 2 * D_MODEL, D_MODEL), 4)]
        return early, [_weighted(proj_v, 4)], last

    def branch_stages(sub, z):
        g0 = (sub % SLOTS) * R
        rows = slice(g0, g0 + R)
        segs = segments(sub)
        c_first = g0 // C
        chunk_ids = list(range(c_first, c_first + R // C))
        w = {}

        def pool():
            parts = []
            for idx, (bi, s0, n) in enumerate(segs):
                ab = z["a"][idx * n:(idx + 1) * n]
                row = lax.broadcasted_iota(jnp.int32, (n, POOL_GW), 0)
                pos1 = row + (t * TT + s0 + pos0 + 1)
                groups = []
                for gi, win in enumerate(POOL_WINDOWS):
                    ls = slice(gi * POOL_GW, (gi + 1) * POOL_GW)
                    base = HIST_ROWS + s0
                    terms = [ext_ref[bi, base - k:base - k + n, ls] for k in range(win)]
                    while len(terms) > 1:
                        terms = [terms[i] + terms[i + 1] for i in range(0, len(terms), 2)]
                    cnt = jnp.minimum(pos1, win).astype(F32)
                    dgi = terms[0] / cnt - ab[:, ls]
                    groups.append(_dot(dgi.astype(BF16), pool_w_ref[0, gi].astype(BF16)))
                parts.append(jnp.concatenate(groups, axis=-1))
            w["ya"] = parts[0] if len(parts) == 1 else jnp.concatenate(parts, axis=0)

        def merge_a():
            y_a = w.pop("ya") * pool_scale_ref[0] * _silu(z.pop("ga"))
            w["merged"] = (_sigmoid(z.pop("gm0"))
                           * _dot(y_a.astype(BF16), w_oa_ref[0]))

        def sgu_norm():
            vb = z.pop("vb")
            mu = jnp.mean(vb, axis=-1, keepdims=True)
            vc = vb - mu
            vn = vc * lax.rsqrt(jnp.mean(vc * vc, axis=-1, keepdims=True) + EPS) * sgu_g_ref[0]
            if emit_vn:
                for idx, (bi, s0, n) in enumerate(segs):
                    vn_ref[bi, s0:s0 + n, :] = vn[idx * n:(idx + 1) * n]
            w["vnb"] = vn.astype(BF16)

        def sgu_mix():
            vnb = w.pop("vnb")
            s_rows = []
            for c0 in range(0, R, L):
                parts = [_dot(wms[g], vnb[c0:c0 + L, g * SGU_GW:(g + 1) * SGU_GW])
                         for g in range(SGU_GROUPS)]
                s_rows.append(jnp.concatenate(parts, axis=-1) + sgu_bias)
            s_all = s_rows[0] if len(s_rows) == 1 else jnp.concatenate(s_rows, axis=0)
            w["yb"] = z.pop("u") * s_all * _silu(z.pop("gb"))

        def merge_b():
            w["merged"] = w["merged"] + (_sigmoid(z.pop("gm1"))
                                         * _dot(w.pop("yb").astype(BF16), w_ob_ref[0]))

        def log_decay():
            pre = _dot(z.pop("lr").astype(BF16), wa2) + ba_ref[0]
            log_sig = jnp.minimum(pre, 0.0) - jnp.log(1.0 + jnp.exp2(jnp.abs(pre) * (-LOG2E)))
            la_ref[rows, :] = log_sig * (LOG2E / GLA_NORMALIZER)

        def cumulative(ci):
            def f():
                la = la_ref[ci * C:(ci + 1) * C, :]
                p0 = la.astype(BF16)
                p1 = (la - p0.astype(F32)).astype(BF16)
                w["b", ci] = _dot(tri, p0) + _dot(tri, p1)
            return f

        def operands(ci):
            def f():
                b_all = w.pop(("b", ci))
                for h in range(GLA_HEADS):
                    ks = slice(h * GLA_DK, (h + 1) * GLA_DK)
                    b = b_all[:, ks]
                    qh = q_ref[ci * C:(ci + 1) * C, ks]
                    kh = k_ref[ci * C:(ci + 1) * C, ks]
                    ends = [b[(i + 1) * GLA_SUB - 1:(i + 1) * GLA_SUB, :] for i in range(nb)]
                    starts = [None] + ends[:-1]
                    ref_start = jnp.concatenate([jnp.zeros((GLA_SUB, GLA_DK), F32)]
                                                + [rows16(e) for e in ends[:-1]], axis=0)
                    ref_end = jnp.concatenate([rows16(e) for e in ends], axis=0)
                    rel = b - ref_start
                    qd = qh * jnp.exp2(rel)
                    kd = kh * jnp.exp2(-rel)
                    ke = kh * jnp.exp2(ref_end - b)

                    def q_from(i, j):
                        if j == i - 1:
                            return blk(qd, i)
                        return blk(qd, i) * rows16(jnp.exp2(starts[i] - ends[j]))

                    def k_to(i, j):
                        if j == i:
                            return blk(ke, i)
                        return blk(ke, i) * rows16(jnp.exp2(ends[j] - ends[i]))

                    pairs = [(qd.astype(BF16), kd.astype(BF16))]
                    for (sz, _) in levels:
                        qblocks, kblocks = [], []
                        for i in range(nb):
                            mid = (i // (2 * sz)) * 2 * sz + sz
                            if (i // sz) % 2 == 1:
                                qblocks.append(q_from(i, mid - 1))
                                kblocks.append(None)
                            else:
                                qblocks.append(None)
                                kblocks.append(k_to(i, mid - 1))
                        pairs.append((stack(qblocks), stack(kblocks)))
                    q_in = stack([blk(qd, 0)]
                                 + [blk(qd, i) * rows16(jnp.exp2(starts[i])) for i in range(1, nb)])
                    k_dec = stack([k_to(i, nb - 1) for i in range(nb)])
                    w["ops", ci, h] = (q_in, k_dec, jnp.exp2(ends[nb - 1]), pairs)
            return f

        def attention(ci):
            def f():
                for h in range(GLA_HEADS):
                    q_in, k_dec, dec, pairs = w.pop(("ops", ci, h))
                    att = jnp.where(mask_diag, _dot_nt(*pairs[0]), 0.0)
                    for (_, same_blk), pr in zip(levels, pairs[1:]):
                        part = _dot_nt(*pr)
                        att = att + (part if same_blk is None else jnp.where(same_blk, part, 0.0))
                    w["att", ci, h] = (q_in, k_dec, dec, att.astype(BF16))
            return f

        def state_chain(ci):
            def f():
                bi = 0 if BB == 1 else ci // chunks_per_seq
                for h in range(GLA_HEADS):
                    q_in, k_dec, dec, att = w.pop(("att", ci, h))
                    vs = slice(h * GLA_DV, (h + 1) * GLA_DV)
                    vh = v_ref[ci * C:(ci + 1) * C, vs]
                    st = st_ref[bi, h]
                    o_ref[ci * C:(ci + 1) * C, vs] = _dot(att, vh) + _dot(q_in, st.astype(BF16))
                    dec_col = jnp.broadcast_to(dec, (GLA_DK, GLA_DK)).T
                    dec_col = jnp.concatenate([dec_col] * (GLA_DV // GLA_DK), axis=1)
                    st_ref[bi, h] = st * dec_col + _dot_tn(k_dec, vh)
            return f

        def merge_c():
            parts = [_rms(o_ref[rows, h * GLA_DV:(h + 1) * GLA_DV], gla_g) for h in range(GLA_HEADS)]
            y_c = jnp.concatenate(parts, axis=-1) * _silu(z.pop("gc"))
            w["merged"] = w["merged"] + (_sigmoid(z.pop("gm2"))
                                         * _dot(y_c.astype(BF16), w_oc_ref[0]))

        def out_proj():
            w["out"] = _dot(w.pop("merged").astype(BF16), w_out_ref[0])

        def residual():
            out = w.pop("out")
            for idx, (bi, s0, n) in enumerate(segs):
                gate = mod_ref[0, bi, 2:3, :]
                y_ref[bi, s0:s0 + n, :] = (x_ref[bi, s0:s0 + n, :]
                                           + _rms(out[idx * n:(idx + 1) * n], post_g * gate))

        early = [_weighted(pool, 2), _weighted(merge_a, 2), _weighted(sgu_norm, 1), _weighted(sgu_mix, 1.5),
                 _weighted(merge_b, 2), _weighted(log_decay, 1)]
        early += [_weighted(cumulative(ci), 0.3) for ci in chunk_ids]
        early += [_weighted(operands(ci), 1.5) for ci in chunk_ids]
        late = [_weighted(attention(ci), 1) for ci in chunk_ids]
        late += [_weighted(state_chain(ci), 1) for ci in chunk_ids]
        late += [_weighted(merge_c, 3, ready=lambda: "gc" in z and "gm2" in z),
                 _weighted(out_proj, 4), _weighted(residual, 2)]
        return early, late

    zs = [dict() for _ in range(NSUB)]
    proj = [projection_stages(sub, zs[sub]) for sub in range(NSUB)]
    branch = [branch_stages(sub, zs[sub]) for sub in range(NSUB)]
    _interleave(proj[0][0])
    for r in range(NSUB + 1):
        matmul_side = ((proj[r - 1][2] if r >= 1 else []) + (proj[r][1] if r < NSUB else [])
                       + (proj[r + 1][0] if r + 1 < NSUB else []))
        _interleave(branch[r - 1][1] if r >= 1 else [], branch[r][0] if r < NSUB else [], matmul_side)

    for bi in range(BB):
        tail = ext_ref[bi, TT:TT + HIST_ROWS, :]
        ext_ref[bi, 0:HIST_ROWS, :] = tail
        hist_ref[bi] = tail

    @pl.when(t == nt - 1)
    def _():
        sfin_ref[...] = st_ref[...]


def _layer_tiles(B, T):
    if T >= SUBTILES_PER_STEP * SUBTILE_ROWS:
        BB, TT, NSUB = 1, SUBTILES_PER_STEP * SUBTILE_ROWS, SUBTILES_PER_STEP
    else:
        BB, TT, NSUB = min(B, SUBTILE_ROWS // T), T, 1
    L = min(T, SGU_LEN)
    C = GLA_CHUNK
    R = BB * TT // NSUB
    assert T % TT == 0 and B % BB == 0 and TT >= HIST_ROWS
    assert (NSUB == 1 or BB == 1) and R % L == 0 and R % C == 0 and TT % C == 0
    return BB, TT, NSUB, L, C


def _vmem_limit(BB, TT, NSUB, weights, has_state):
    M = BB * TT
    ring = min(NSUB, SUBTILE_SLOTS) * (M // NSUB)
    wbytes = sum(math.prod(w.shape[1:]) * w.dtype.itemsize for w in weights)
    io = 2 * 2 * M * D_MODEL * 4
    state = 2 * (2 if has_state else 1) * BB * (GLA_HEADS * GLA_DK * GLA_DV + HIST_ROWS * D_A) * 4
    scratch = ring * (D_MODEL * 2 + 3 * D_CK * 4 + D_CV * 2 + D_CV * 4) + BB * (TT + HIST_ROWS) * D_A * 4 \
        + BB * GLA_HEADS * GLA_DK * GLA_DV * 4
    temps = ring * (3 * D_MODEL + 6 * D_MODEL) * 4
    return min(wbytes + io + state + scratch + temps, V7X_VMEM_BYTES - VMEM_RESERVE_BYTES)


def _mixer_layer(l, x, mod, mod_row0, state, weights, pos0, emit_vn):
    B, T, D = x.shape
    BB, TT, NSUB, L, C = _layer_tiles(B, T)
    ring = min(NSUB, SUBTILE_SLOTS) * (BB * TT // NSUB)
    has_state = state is not None
    mod_blk0 = mod_row0 // BB

    def wspec(w):
        nd = w.ndim
        return pl.BlockSpec((1,) + w.shape[1:], lambda b, t: (l,) + (0,) * (nd - 1),
                            pipeline_mode=pl.Buffered(1))

    grid = (B // BB, T // TT)
    inputs = [x, mod]
    in_specs = [
        pl.BlockSpec((BB, TT, D), lambda b, t: (b, t, 0)),
        pl.BlockSpec((1, BB, 3, D), lambda b, t: (l, mod_blk0 + b, 0, 0)),
    ]
    if has_state:
        inputs += list(state)
        in_specs += [
            pl.BlockSpec((1, BB, HIST_ROWS, D_A), lambda b, t: (l, b, 0, 0)),
            pl.BlockSpec((1, BB, GLA_HEADS, GLA_DK, GLA_DV), lambda b, t: (l, b, 0, 0, 0)),
        ]
    inputs += list(weights)
    in_specs += [wspec(w) for w in weights]
    out_shape = [
        jax.ShapeDtypeStruct((B, T, D), F32),
        jax.ShapeDtypeStruct((B, HIST_ROWS, D_A), F32),
        jax.ShapeDtypeStruct((B, GLA_HEADS, GLA_DK, GLA_DV), F32),
    ]
    out_specs = [
        pl.BlockSpec((BB, TT, D), lambda b, t: (b, t, 0)),
        pl.BlockSpec((BB, HIST_ROWS, D_A), lambda b, t: (b, 0, 0)),
        pl.BlockSpec((BB, GLA_HEADS, GLA_DK, GLA_DV), lambda b, t: (b, 0, 0, 0)),
    ]
    if emit_vn:
        out_shape.append(jax.ShapeDtypeStruct((B, T, D_B), F32))
        out_specs.append(pl.BlockSpec((BB, TT, D_B), lambda b, t: (b, t, 0)))
    scratch_shapes = [
        pltpu.VMEM((ring, D), BF16),
        pltpu.VMEM((BB, HIST_ROWS + TT, D_A), F32),
        pltpu.VMEM((ring, D_CK), F32),
        pltpu.VMEM((ring, D_CK), F32),
        pltpu.VMEM((ring, D_CV), BF16),
        pltpu.VMEM((ring, D_CK), F32),
        pltpu.VMEM((ring, D_CV), F32),
        pltpu.VMEM((BB, GLA_HEADS, GLA_DK, GLA_DV), F32),
    ]
    body = functools.partial(_layer_kernel, BB=BB, TT=TT, NSUB=NSUB, L=L, C=C, pos0=pos0,
                             has_state=has_state, emit_vn=emit_vn)
    return pl.pallas_call(
        body,
        grid=grid,
        in_specs=in_specs,
        out_specs=out_specs,
        out_shape=out_shape,
        scratch_shapes=scratch_shapes,
        compiler_params=pltpu.CompilerParams(
            dimension_semantics=("arbitrary", "arbitrary"),
            vmem_limit_bytes=_vmem_limit(BB, TT, NSUB, weights, has_state)),
        name="mixer_layer_T%d" % T,
    )(*inputs)


def kernel(x_prompt, x_sample, state_pool, state_gla, c_prompt, c_sample, ada_w, ada_b, pre_norm_g,
           post_norm_g, w_in, pool_w, pool_scale, sgu_norm_g, sgu_w, sgu_b, gla_wa2, gla_ba, gla_norm_g,
           w_oa, w_ob, w_oc, w_out):
    bp = x_prompt.shape[0]
    bs = x_sample.shape[0]
    mod = _ada_modulation(jnp.concatenate([c_prompt, c_sample], axis=0), ada_w, ada_b)
    mod = mod.reshape(DEPTH, bp + bs, 3, D_MODEL)

    w_pack = jnp.swapaxes(w_in, 1, 2).astype(BF16)
    sgu_bias = jnp.repeat(jnp.swapaxes(sgu_b, 1, 2), SGU_GW, axis=2)
    vec = lambda v: v[:, None, :]
    weights = [vec(pre_norm_g), vec(post_norm_g), w_pack, pool_w, vec(pool_scale), vec(sgu_norm_g),
               sgu_w, sgu_bias, gla_wa2, vec(gla_ba), vec(gla_norm_g),
               w_oa.astype(BF16), w_ob.astype(BF16), w_oc.astype(BF16), w_out.astype(BF16)]
    hist_sample = jnp.pad(state_pool, ((0, 0), (0, 0), (HIST_ROWS - POOL_HIST, 0), (0, 0)))

    xp, xs = x_prompt, x_sample
    pool_p, gla_p, pool_s, gla_s, sgu_s = [], [], [], [], []
    for l in range(DEPTH):
        xp, hp, sp = _mixer_layer(l, xp, mod, 0, None, weights, 0, False)
        xs, hs, ss, vs = _mixer_layer(l, xs, mod, bp, (hist_sample, state_gla), weights, PAST_LEN, True)
        pool_p.append(hp[:, HIST_ROWS - POOL_HIST:])
        gla_p.append(sp)
        pool_s.append(hs[:, HIST_ROWS - POOL_HIST:])
        gla_s.append(ss)
        sgu_s.append(vs)
    return (xp, xs, jnp.stack(pool_p), jnp.stack(gla_p), jnp.stack(pool_s), jnp.stack(gla_s),
            jnp.stack(sgu_s))
```

```python
import functools
import math

import jax
import jax.numpy as jnp
from jax import lax
from jax.experimental import pallas as pl
from jax.experimental.pallas import tpu as pltpu

D_MODEL = 1024
DEPTH = 2
PAST_LEN = 2048
EPS = 1e-6
POOL_WINDOWS = (2, 4, 8, 16)
POOL_GW = 128
D_A = 512
POOL_HIST = 15
HIST_ROWS = 16
SGU_LEN = 128
SGU_GROUPS = 4
D_B = 512
SGU_GW = 128
GLA_HEADS = 4
GLA_DK = 128
GLA_DV = 256
D_CK = 512
D_CV = 1024
GLA_RANK = 16
GLA_NORMALIZER = 16.0
GLA_SUB = 16
GLA_CHUNK = 128
SUBTILE_ROWS = 256
SUBTILES_PER_STEP = 2
SUBTILE_SLOTS = 2
ADA_COL_BLOCKS = 3
LANE = 128
V7X_VMEM_BYTES = 64 * 1024 * 1024
VMEM_RESERVE_BYTES = 4 * 1024 * 1024
LOG2E = math.log2(math.e)

OFF_A, OFF_GA, OFF_U, OFF_VB, OFF_GB = 0, 512, 1024, 1536, 2048
OFF_Q, OFF_K, OFF_VC, OFF_GC = 2560, 3072, 3584, 4608
OFF_LR = 5632
OFF_GM = OFF_LR + GLA_RANK

F32 = jnp.float32
BF16 = jnp.bfloat16


def _dot(a, b):
    return jnp.dot(a, b, preferred_element_type=F32)


def _dot_nt(a, b):
    return lax.dot_general(a, b, (((1,), (1,)), ((), ())), preferred_element_type=F32)


def _dot_tn(a, b):
    return lax.dot_general(a, b, (((0,), (0,)), ((), ())), preferred_element_type=F32)


def _sigmoid(x):
    return 1.0 / (1.0 + jnp.exp2(x * (-LOG2E)))


def _silu(x):
    return x * _sigmoid(x)


def _rms(x, g):
    return x * lax.rsqrt(jnp.mean(x * x, axis=-1, keepdims=True) + EPS) * g


def _ada_kernel(c_ref, w_ref, b_ref, o_ref):
    c = c_ref[...]
    o_ref[0] = _dot(_silu(c).astype(BF16), w_ref[0].astype(BF16)) + b_ref[0]


def _ada_modulation(c_all, ada_w, ada_b):
    n = c_all.shape[0]
    return pl.pallas_call(
        _ada_kernel,
        grid=(DEPTH, ADA_COL_BLOCKS),
        in_specs=[
            pl.BlockSpec((n, D_MODEL), lambda l, j: (0, 0)),
            pl.BlockSpec((1, D_MODEL, D_MODEL), lambda l, j: (l, 0, j)),
            pl.BlockSpec((1, 1, D_MODEL), lambda l, j: (l, 0, j)),
        ],
        out_specs=pl.BlockSpec((1, n, D_MODEL), lambda l, j: (l, 0, j)),
        out_shape=jax.ShapeDtypeStruct((DEPTH, n, 3 * D_MODEL), F32),
        name="ada_modulation",
    )(c_all, ada_w, ada_b.reshape(DEPTH, 1, 3 * D_MODEL))


def _interleave(*stage_lists):
    lists = [sl for sl in stage_lists if sl]
    done = [0] * len(lists)
    while any(d < len(sl) for d, sl in zip(done, lists)):
        k = min((i for i in range(len(lists)) if done[i] < len(lists[i])),
                key=lambda i: done[i] / len(lists[i]))
        lists[k][done[k]]()
        done[k] += 1


def _layer_kernel(*refs, BB, TT, NSUB, L, C, pos0, has_state, emit_vn):
    refs = list(refs)
    x_ref, mod_ref = refs[:2]
    del refs[:2]
    if has_state:
        hist0_ref, s0_ref = refs[:2]
        del refs[:2]
    (pre_g_ref, post_g_ref, w_in_ref, pool_w_ref, pool_scale_ref, sgu_g_ref, sgu_w_ref, sgu_b_ref,
     wa2_ref, ba_ref, gla_g_ref, w_oa_ref, w_ob_ref, w_oc_ref, w_out_ref) = refs[:15]
    del refs[:15]
    y_ref, hist_ref, sfin_ref = refs[:3]
    del refs[:3]
    if emit_vn:
        vn_ref = refs.pop(0)
    hb_ref, ext_ref, q_ref, k_ref, v_ref, la_ref, o_ref, st_ref = refs

    M = BB * TT
    R = M // NSUB
    SLOTS = min(NSUB, SUBTILE_SLOTS)
    t = pl.program_id(1)
    nt = pl.num_programs(1)

    @pl.when(t == 0)
    def _():
        if has_state:
            ext_ref[:, 0:HIST_ROWS, :] = hist0_ref[0]
            st_ref[...] = s0_ref[0]
        else:
            ext_ref[:, 0:HIST_ROWS, :] = jnp.zeros((BB, HIST_ROWS, D_A), F32)
            st_ref[...] = jnp.zeros(st_ref.shape, F32)

    pre_g = pre_g_ref[0]
    post_g = post_g_ref[0]
    gla_g = gla_g_ref[0]
    ri = lax.broadcasted_iota(jnp.int32, (L, L), 0)
    ci_ = lax.broadcasted_iota(jnp.int32, (L, L), 1)
    wms = [jnp.where(ri >= ci_, sgu_w_ref[0, g, 0:L, 0:L], 0.0).astype(BF16) for g in range(SGU_GROUPS)]
    sgu_bias = sgu_b_ref[0, 0:L, :]
    wa2 = jnp.concatenate([wa2_ref[0], jnp.zeros((LANE - GLA_RANK, D_CK), F32)], axis=0).astype(BF16)
    cr = lax.broadcasted_iota(jnp.int32, (C, C), 0)
    cc = lax.broadcasted_iota(jnp.int32, (C, C), 1)
    tri = (cr >= cc).astype(BF16)
    sub_shift = GLA_SUB.bit_length() - 1
    mask_diag = (cr >= cc) & ((cr >> sub_shift) == (cc >> sub_shift))
    nb = C // GLA_SUB
    levels = []
    s = 1
    while 2 * s <= nb:
        sh = sub_shift + s.bit_length()
        levels.append((s, None if 2 * s == nb else (cr >> sh) == (cc >> sh)))
        s *= 2
    chunks_per_seq = TT // C

    def blk(x, i):
        return x[i * GLA_SUB:(i + 1) * GLA_SUB]

    def rows16(r):
        return jnp.broadcast_to(r, (GLA_SUB, r.shape[1]))

    def stack(blocks):
        zero = jnp.zeros((GLA_SUB, GLA_DK), F32)
        return jnp.concatenate([zero if b is None else b for b in blocks], axis=0).astype(BF16)

    def segments(sub):
        if BB == 1:
            return [(0, sub * R, R)]
        return [(bi, 0, TT) for bi in range(BB)]

    def projection_stages(sub, z):
        g0 = (sub % SLOTS) * R
        rows = slice(g0, g0 + R)

        def norm():
            for idx, (bi, s0, n) in enumerate(segments(sub)):
                xb = x_ref[bi, s0:s0 + n, :]
                shift = mod_ref[0, bi, 0:1, :]
                scale = mod_ref[0, bi, 1:2, :]
                hmod = _rms(xb, pre_g * (1.0 + scale)) + shift
                hb_ref[g0 + idx * n:g0 + (idx + 1) * n, :] = hmod.astype(BF16)

        def proj(off, width):
            return _dot_nt(hb_ref[rows, :], w_in_ref[0, off:off + width, :])

        def keep(name, off, width):
            def f():
                z[name] = proj(off, width)
            return f

        def proj_a():
            a = proj(OFF_A, D_A)
            z["a"] = a
            for idx, (bi, s0, n) in enumerate(segments(sub)):
                ext_ref[bi, HIST_ROWS + s0:HIST_ROWS + s0 + n, :] = a[idx * n:(idx + 1) * n]

        def proj_q():
            q_ref[rows, :] = proj(OFF_Q, D_CK) * (GLA_DK ** -0.5)

        def proj_k():
            k_ref[rows, :] = proj(OFF_K, D_CK)

        def proj_v():
            v_ref[rows, :] = proj(OFF_VC, D_CV).astype(BF16)

        early = [norm, proj_a, keep("ga", OFF_GA, D_A), keep("gm0", OFF_GM, D_MODEL),
                 keep("vb", OFF_VB, D_B), keep("u", OFF_U, D_B), keep("gb", OFF_GB, D_B),
                 keep("gm1", OFF_GM + D_MODEL, D_MODEL), keep("lr", OFF_LR, LANE), proj_q, proj_k]
        last = [keep("gc", OFF_GC, D_CV), keep("gm2", OFF_GM + 2 * D_MODEL, D_MODEL)]
        return early, [proj_v], last

    def branch_stages(sub, z):
        g0 = (sub % SLOTS) * R
        rows = slice(g0, g0 + R)
        segs = segments(sub)
        c_first = g0 // C
        chunk_ids = list(range(c_first, c_first + R // C))
        w = {}

        def pool():
            parts = []
            for idx, (bi, s0, n) in enumerate(segs):
                ab = z["a"][idx * n:(idx + 1) * n]
                row = lax.broadcasted_iota(jnp.int32, (n, POOL_GW), 0)
                pos1 = row + (t * TT + s0 + pos0 + 1)
                groups = []
                for gi, win in enumerate(POOL_WINDOWS):
                    ls = slice(gi * POOL_GW, (gi + 1) * POOL_GW)
                    base = HIST_ROWS + s0
                    terms = [ext_ref[bi, base - k:base - k + n, ls] for k in range(win)]
                    while len(terms) > 1:
                        terms = [terms[i] + terms[i + 1] for i in range(0, len(terms), 2)]
                    cnt = jnp.minimum(pos1, win).astype(F32)
                    dgi = terms[0] / cnt - ab[:, ls]
                    groups.append(_dot(dgi.astype(BF16), pool_w_ref[0, gi].astype(BF16)))
                parts.append(jnp.concatenate(groups, axis=-1))
            w["ya"] = parts[0] if len(parts) == 1 else jnp.concatenate(parts, axis=0)

        def merge_a():
            y_a = w.pop("ya") * pool_scale_ref[0] * _silu(z.pop("ga"))
            w["merged"] = (_sigmoid(z.pop("gm0"))
                           * _dot(y_a.astype(BF16), w_oa_ref[0]))

        def sgu_norm():
            vb = z.pop("vb")
            mu = jnp.mean(vb, axis=-1, keepdims=True)
            vc = vb - mu
            vn = vc * lax.rsqrt(jnp.mean(vc * vc, axis=-1, keepdims=True) + EPS) * sgu_g_ref[0]
            if emit_vn:
                for idx, (bi, s0, n) in enumerate(segs):
                    vn_ref[bi, s0:s0 + n, :] = vn[idx * n:(idx + 1) * n]
            w["vnb"] = vn.astype(BF16)

        def sgu_mix():
            vnb = w.pop("vnb")
            s_rows = []
            for c0 in range(0, R, L):
                parts = [_dot(wms[g], vnb[c0:c0 + L, g * SGU_GW:(g + 1) * SGU_GW])
                         for g in range(SGU_GROUPS)]
                s_rows.append(jnp.concatenate(parts, axis=-1) + sgu_bias)
            s_all = s_rows[0] if len(s_rows) == 1 else jnp.concatenate(s_rows, axis=0)
            w["yb"] = z.pop("u") * s_all * _silu(z.pop("gb"))

        def merge_b():
            w["merged"] = w["merged"] + (_sigmoid(z.pop("gm1"))
                                         * _dot(w.pop("yb").astype(BF16), w_ob_ref[0]))

        def log_decay():
            pre = _dot(z.pop("lr").astype(BF16), wa2) + ba_ref[0]
            log_sig = jnp.minimum(pre, 0.0) - jnp.log(1.0 + jnp.exp2(jnp.abs(pre) * (-LOG2E)))
            la_ref[rows, :] = log_sig * (LOG2E / GLA_NORMALIZER)

        def cumulative(ci):
            def f():
                la = la_ref[ci * C:(ci + 1) * C, :]
                p0 = la.astype(BF16)
                p1 = (la - p0.astype(F32)).astype(BF16)
                w["b", ci] = _dot(tri, p0) + _dot(tri, p1)
            return f

        def operands(ci):
            def f():
                b_all = w.pop(("b", ci))
                for h in range(GLA_HEADS):
                    ks = slice(h * GLA_DK, (h + 1) * GLA_DK)
                    b = b_all[:, ks]
                    qh = q_ref[ci * C:(ci + 1) * C, ks]
                    kh = k_ref[ci * C:(ci + 1) * C, ks]
                    ends = [b[(i + 1) * GLA_SUB - 1:(i + 1) * GLA_SUB, :] for i in range(nb)]
                    starts = [None] + ends[:-1]
                    ref_start = jnp.concatenate([jnp.zeros((GLA_SUB, GLA_DK), F32)]
                                                + [rows16(e) for e in ends[:-1]], axis=0)
                    ref_end = jnp.concatenate([rows16(e) for e in ends], axis=0)
                    rel = b - ref_start
                    qd = qh * jnp.exp2(rel)
                    kd = kh * jnp.exp2(-rel)
                    ke = kh * jnp.exp2(ref_end - b)

                    def q_from(i, j):
                        if j == i - 1:
                            return blk(qd, i)
                        return blk(qd, i) * rows16(jnp.exp2(starts[i] - ends[j]))

                    def k_to(i, j):
                        if j == i:
                            return blk(ke, i)
                        return blk(ke, i) * rows16(jnp.exp2(ends[j] - ends[i]))

                    pairs = [(qd.astype(BF16), kd.astype(BF16))]
                    for (sz, _) in levels:
                        qblocks, kblocks = [], []
                        for i in range(nb):
                            mid = (i // (2 * sz)) * 2 * sz + sz
                            if (i // sz) % 2 == 1:
                                qblocks.append(q_from(i, mid - 1))
                                kblocks.append(None)
                            else:
                                qblocks.append(None)
                                kblocks.append(k_to(i, mid - 1))
                        pairs.append((stack(qblocks), stack(kblocks)))
                    q_in = stack([blk(qd, 0)]
                                 + [blk(qd, i) * rows16(jnp.exp2(starts[i])) for i in range(1, nb)])
                    k_dec = stack([k_to(i, nb - 1) for i in range(nb)])
                    w["ops", ci, h] = (q_in, k_dec, jnp.exp2(ends[nb - 1]), pairs)
            return f

        def attention(ci):
            def f():
                for h in range(GLA_HEADS):
                    q_in, k_dec, dec, pairs = w.pop(("ops", ci, h))
                    att = jnp.where(mask_diag, _dot_nt(*pairs[0]), 0.0)
                    for (_, same_blk), pr in zip(levels, pairs[1:]):
                        part = _dot_nt(*pr)
                        att = att + (part if same_blk is None else jnp.where(same_blk, part, 0.0))
                    w["att", ci, h] = (q_in, k_dec, dec, att.astype(BF16))
            return f

        def state_chain(ci):
            def f():
                bi = 0 if BB == 1 else ci // chunks_per_seq
                for h in range(GLA_HEADS):
                    q_in, k_dec, dec, att = w.pop(("att", ci, h))
                    vs = slice(h * GLA_DV, (h + 1) * GLA_DV)
                    vh = v_ref[ci * C:(ci + 1) * C, vs]
                    st = st_ref[bi, h]
                    o_ref[ci * C:(ci + 1) * C, vs] = _dot(att, vh) + _dot(q_in, st.astype(BF16))
                    dec_col = jnp.broadcast_to(dec, (GLA_DK, GLA_DK)).T
                    dec_col = jnp.concatenate([dec_col] * (GLA_DV // GLA_DK), axis=1)
                    st_ref[bi, h] = st * dec_col + _dot_tn(k_dec, vh)
            return f

        def merge_c():
            parts = [_rms(o_ref[rows, h * GLA_DV:(h + 1) * GLA_DV], gla_g) for h in range(GLA_HEADS)]
            y_c = jnp.concatenate(parts, axis=-1) * _silu(z.pop("gc"))
            w["merged"] = w["merged"] + (_sigmoid(z.pop("gm2"))
                                         * _dot(y_c.astype(BF16), w_oc_ref[0]))

        def out_proj():
            w["out"] = _dot(w.pop("merged").astype(BF16), w_out_ref[0])

        def residual():
            out = w.pop("out")
            for idx, (bi, s0, n) in enumerate(segs):
                gate = mod_ref[0, bi, 2:3, :]
                y_ref[bi, s0:s0 + n, :] = (x_ref[bi, s0:s0 + n, :]
                                           + _rms(out[idx * n:(idx + 1) * n], post_g * gate))

        early = [pool, merge_a, sgu_norm, sgu_mix, merge_b, log_decay]
        early += [cumulative(ci) for ci in chunk_ids]
        early += [operands(ci) for ci in chunk_ids]
        late = [attention(ci) for ci in chunk_ids]
        late += [state_chain(ci) for ci in chunk_ids]
        late += [merge_c, out_proj, residual]
        return early, late

    zs = [dict() for _ in range(NSUB)]
    proj = [projection_stages(sub, zs[sub]) for sub in range(NSUB)]
    branch = [branch_stages(sub, zs[sub]) for sub in range(NSUB)]
    _interleave(proj[0][0])
    for r in range(NSUB + 1):
        matmul_side = ((proj[r - 1][2] if r >= 1 else []) + (proj[r][1] if r < NSUB else [])
                       + (proj[r + 1][0] if r + 1 < NSUB else []))
        _interleave(branch[r - 1][1] if r >= 1 else [], branch[r][0] if r < NSUB else [], matmul_side)

    for bi in range(BB):
        tail = ext_ref[bi, TT:TT + HIST_ROWS, :]
        ext_ref[bi, 0:HIST_ROWS, :] = tail
        hist_ref[bi] = tail

    @pl.when(t == nt - 1)
    def _():
        sfin_ref[...] = st_ref[...]


def _layer_tiles(B, T):
    if T >= SUBTILES_PER_STEP * SUBTILE_ROWS:
        BB, TT, NSUB = 1, SUBTILES_PER_STEP * SUBTILE_ROWS, SUBTILES_PER_STEP
    else:
        BB, TT, NSUB = min(B, SUBTILE_ROWS // T), T, 1
    L = min(T, SGU_LEN)
    C = min(GLA_CHUNK, TT)
    R = BB * TT // NSUB
    assert T % TT == 0 and B % BB == 0 and TT >= HIST_ROWS
    assert (NSUB == 1 or BB == 1) and R % L == 0 and R % C == 0 and TT % C == 0
    return BB, TT, NSUB, L, C


def _vmem_limit(BB, TT, NSUB, weights, has_state):
    M = BB * TT
    ring = min(NSUB, SUBTILE_SLOTS) * (M // NSUB)
    wbytes = sum(math.prod(w.shape[1:]) * w.dtype.itemsize for w in weights)
    io = 2 * 2 * M * D_MODEL * 4
    state = 2 * (2 if has_state else 1) * BB * (GLA_HEADS * GLA_DK * GLA_DV + HIST_ROWS * D_A) * 4
    scratch = ring * (D_MODEL * 2 + 3 * D_CK * 4 + D_CV * 2 + D_CV * 4) + BB * (TT + HIST_ROWS) * D_A * 4 \
        + BB * GLA_HEADS * GLA_DK * GLA_DV * 4
    temps = ring * (3 * D_MODEL + 6 * D_MODEL) * 4
    return min(wbytes + io + state + scratch + temps, V7X_VMEM_BYTES - VMEM_RESERVE_BYTES)


def _mixer_layer(l, x, mod, mod_row0, state, weights, pos0, emit_vn):
    B, T, D = x.shape
    BB, TT, NSUB, L, C = _layer_tiles(B, T)
    ring = min(NSUB, SUBTILE_SLOTS) * (BB * TT // NSUB)
    has_state = state is not None
    mod_blk0 = mod_row0 // BB

    def wspec(w):
        nd = w.ndim
        return pl.BlockSpec((1,) + w.shape[1:], lambda b, t: (l,) + (0,) * (nd - 1),
                            pipeline_mode=pl.Buffered(1))

    grid = (B // BB, T // TT)
    inputs = [x, mod]
    in_specs = [
        pl.BlockSpec((BB, TT, D), lambda b, t: (b, t, 0)),
        pl.BlockSpec((1, BB, 3, D), lambda b, t: (l, mod_blk0 + b, 0, 0)),
    ]
    if has_state:
        inputs += list(state)
        in_specs += [
            pl.BlockSpec((1, BB, HIST_ROWS, D_A), lambda b, t: (l, b, 0, 0)),
            pl.BlockSpec((1, BB, GLA_HEADS, GLA_DK, GLA_DV), lambda b, t: (l, b, 0, 0, 0)),
        ]
    inputs += list(weights)
    in_specs += [wspec(w) for w in weights]
    out_shape = [
        jax.ShapeDtypeStruct((B, T, D), F32),
        jax.ShapeDtypeStruct((B, HIST_ROWS, D_A), F32),
        jax.ShapeDtypeStruct((B, GLA_HEADS, GLA_DK, GLA_DV), F32),
    ]
    out_specs = [
        pl.BlockSpec((BB, TT, D), lambda b, t: (b, t, 0)),
        pl.BlockSpec((BB, HIST_ROWS, D_A), lambda b, t: (b, 0, 0)),
        pl.BlockSpec((BB, GLA_HEADS, GLA_DK, GLA_DV), lambda b, t: (b, 0, 0, 0)),
    ]
    if emit_vn:
        out_shape.append(jax.ShapeDtypeStruct((B, T, D_B), F32))
        out_specs.append(pl.BlockSpec((BB, TT, D_B), lambda b, t: (b, t, 0)))
    scratch_shapes = [
        pltpu.VMEM((ring, D), BF16),
        pltpu.VMEM((BB, HIST_ROWS + TT, D_A), F32),
        pltpu.VMEM((ring, D_CK), F32),
        pltpu.VMEM((ring, D_CK), F32),
        pltpu.VMEM((ring, D_CV), BF16),
        pltpu.VMEM((ring, D_CK), F32),
        pltpu.VMEM((ring, D_CV), F32),
        pltpu.VMEM((BB, GLA_HEADS, GLA_DK, GLA_DV), F32),
    ]
    body = functools.partial(_layer_kernel, BB=BB, TT=TT, NSUB=NSUB, L=L, C=C, pos0=pos0,
                             has_state=has_state, emit_vn=emit_vn)
    return pl.pallas_call(
        body,
        grid=grid,
        in_specs=in_specs,
        out_specs=out_specs,
        out_shape=out_shape,
        scratch_shapes=scratch_shapes,
        compiler_params=pltpu.CompilerParams(
            dimension_semantics=("arbitrary", "arbitrary"),
            vmem_limit_bytes=_vmem_limit(BB, TT, NSUB, weights, has_state)),
        name="mixer_layer_T%d" % T,
    )(*inputs)


def kernel(x_prompt, x_sample, state_pool, state_gla, c_prompt, c_sample, ada_w, ada_b, pre_norm_g,
           post_norm_g, w_in, pool_w, pool_scale, sgu_norm_g, sgu_w, sgu_b, gla_wa2, gla_ba, gla_norm_g,
           w_oa, w_ob, w_oc, w_out):
    bp = x_prompt.shape[0]
    bs = x_sample.shape[0]
    mod = _ada_modulation(jnp.concatenate([c_prompt, c_sample], axis=0), ada_w, ada_b)
    mod = mod.reshape(DEPTH, bp + bs, 3, D_MODEL)

    w_pack = jnp.swapaxes(w_in, 1, 2).astype(BF16)
    sgu_bias = jnp.repeat(jnp.swapaxes(sgu_b, 1, 2), SGU_GW, axis=2)
    vec = lambda v: v[:, None, :]
    weights = [vec(pre_norm_g), vec(post_norm_g), w_pack, pool_w, vec(pool_scale), vec(sgu_norm_g),
               sgu_w, sgu_bias, gla_wa2, vec(gla_ba), vec(gla_norm_g),
               w_oa.astype(BF16), w_ob.astype(BF16), w_oc.astype(BF16), w_out.astype(BF16)]
    hist_sample = jnp.pad(state_pool, ((0, 0), (0, 0), (HIST_ROWS - POOL_HIST, 0), (0, 0)))

    xp, xs = x_prompt, x_sample
    pool_p, gla_p, pool_s, gla_s, sgu_s = [], [], [], [], []
    for l in range(DEPTH):
        xp, hp, sp = _mixer_layer(l, xp, mod, 0, None, weights, 0, False)
        xs, hs, ss, vs = _mixer_layer(l, xs, mod, bp, (hist_sample, state_gla), weights, PAST_LEN, True)
        pool_p.append(hp[:, HIST_ROWS - POOL_HIST:])
        gla_p.append(sp)
        pool_s.append(hs[:, HIST_ROWS - POOL_HIST:])
        gla_s.append(ss)
        sgu_s.append(vs)
    return (xp, xs, jnp.stack(pool_p), jnp.stack(gla_p), jnp.stack(pool_s), jnp.stack(gla_s),
            jnp.stack(sgu_s))
```

```python
import functools
import math

import jax
import jax.numpy as jnp
from jax import lax
from jax.experimental import pallas as pl
from jax.experimental.pallas import tpu as pltpu

D_MODEL = 1024
DEPTH = 2
PAST_LEN = 2048
EPS = 1e-6
POOL_WINDOWS = (2, 4, 8, 16)
POOL_GW = 128
D_A = 512
POOL_HIST = 15
HIST_ROWS = 16
SGU_LEN = 128
SGU_GROUPS = 4
D_B = 512
SGU_GW = 128
GLA_HEADS = 4
GLA_DK = 128
GLA_DV = 256
D_CK = 512
D_CV = 1024
GLA_RANK = 16
GLA_NORMALIZER = 16.0
GLA_SUB = 16
GLA_CHUNK = 128
SUBTILE_ROWS = 256
SUBTILES_PER_STEP = 2
SUBTILE_SLOTS = 2
ADA_COL_BLOCKS = 3
LANE = 128
V7X_VMEM_BYTES = 64 * 1024 * 1024
VMEM_RESERVE_BYTES = 4 * 1024 * 1024
LOG2E = math.log2(math.e)

OFF_A, OFF_GA, OFF_U, OFF_VB, OFF_GB = 0, 512, 1024, 1536, 2048
OFF_Q, OFF_K, OFF_VC, OFF_GC = 2560, 3072, 3584, 4608
OFF_LR = 5632
OFF_GM = OFF_LR + GLA_RANK

F32 = jnp.float32
BF16 = jnp.bfloat16


def _dot(a, b):
    return jnp.dot(a, b, preferred_element_type=F32)


def _dot_nt(a, b):
    return lax.dot_general(a, b, (((1,), (1,)), ((), ())), preferred_element_type=F32)


def _dot_tn(a, b):
    return lax.dot_general(a, b, (((0,), (0,)), ((), ())), preferred_element_type=F32)


def _sigmoid(x):
    return 1.0 / (1.0 + jnp.exp2(x * (-LOG2E)))


def _silu(x):
    return x * _sigmoid(x)


def _rms(x, g):
    return x * lax.rsqrt(jnp.mean(x * x, axis=-1, keepdims=True) + EPS) * g


def _ada_kernel(c_ref, w_ref, b_ref, o_ref):
    c = c_ref[...]
    o_ref[0] = _dot(_silu(c).astype(BF16), w_ref[0].astype(BF16)) + b_ref[0]


def _ada_modulation(c_all, ada_w, ada_b):
    n = c_all.shape[0]
    return pl.pallas_call(
        _ada_kernel,
        grid=(DEPTH, ADA_COL_BLOCKS),
        in_specs=[
            pl.BlockSpec((n, D_MODEL), lambda l, j: (0, 0)),
            pl.BlockSpec((1, D_MODEL, D_MODEL), lambda l, j: (l, 0, j)),
            pl.BlockSpec((1, 1, D_MODEL), lambda l, j: (l, 0, j)),
        ],
        out_specs=pl.BlockSpec((1, n, D_MODEL), lambda l, j: (l, 0, j)),
        out_shape=jax.ShapeDtypeStruct((DEPTH, n, 3 * D_MODEL), F32),
        name="ada_modulation",
    )(c_all, ada_w, ada_b.reshape(DEPTH, 1, 3 * D_MODEL))


def _interleave(*stage_lists):
    lists = [sl for sl in stage_lists if sl]
    done = [0] * len(lists)
    while any(d < len(sl) for d, sl in zip(done, lists)):
        k = min((i for i in range(len(lists)) if done[i] < len(lists[i])),
                key=lambda i: done[i] / len(lists[i]))
        lists[k][done[k]]()
        done[k] += 1


def _layer_kernel(*refs, BB, TT, NSUB, L, C, pos0, has_state, emit_vn):
    refs = list(refs)
    x_ref, mod_ref = refs[:2]
    del refs[:2]
    if has_state:
        hist0_ref, s0_ref = refs[:2]
        del refs[:2]
    (pre_g_ref, post_g_ref, w_in_ref, pool_w_ref, pool_scale_ref, sgu_g_ref, sgu_w_ref, sgu_b_ref,
     wa2_ref, ba_ref, gla_g_ref, w_oa_ref, w_ob_ref, w_oc_ref, w_out_ref) = refs[:15]
    del refs[:15]
    y_ref, hist_ref, sfin_ref = refs[:3]
    del refs[:3]
    if emit_vn:
        vn_ref = refs.pop(0)
    hb_ref, ext_ref, q_ref, k_ref, v_ref, la_ref, o_ref, st_ref = refs

    M = BB * TT
    R = M // NSUB
    SLOTS = min(NSUB, SUBTILE_SLOTS)
    t = pl.program_id(1)
    nt = pl.num_programs(1)

    @pl.when(t == 0)
    def _():
        if has_state:
            ext_ref[:, 0:HIST_ROWS, :] = hist0_ref[0]
            st_ref[...] = s0_ref[0]
        else:
            ext_ref[:, 0:HIST_ROWS, :] = jnp.zeros((BB, HIST_ROWS, D_A), F32)
            st_ref[...] = jnp.zeros(st_ref.shape, F32)

    pre_g = pre_g_ref[0]
    post_g = post_g_ref[0]
    gla_g = gla_g_ref[0]
    ri = lax.broadcasted_iota(jnp.int32, (L, L), 0)
    ci_ = lax.broadcasted_iota(jnp.int32, (L, L), 1)
    wms = [jnp.where(ri >= ci_, sgu_w_ref[0, g, 0:L, 0:L], 0.0).astype(BF16) for g in range(SGU_GROUPS)]
    sgu_bias = sgu_b_ref[0, 0:L, :]
    wa2 = jnp.concatenate([wa2_ref[0], jnp.zeros((LANE - GLA_RANK, D_CK), F32)], axis=0).astype(BF16)
    cr = lax.broadcasted_iota(jnp.int32, (C, C), 0)
    cc = lax.broadcasted_iota(jnp.int32, (C, C), 1)
    tri = (cr >= cc).astype(BF16)
    sub_shift = GLA_SUB.bit_length() - 1
    mask_diag = (cr >= cc) & ((cr >> sub_shift) == (cc >> sub_shift))
    nb = C // GLA_SUB
    levels = []
    s = 1
    while 2 * s <= nb:
        sh = sub_shift + s.bit_length()
        levels.append((s, None if 2 * s == nb else (cr >> sh) == (cc >> sh)))
        s *= 2
    chunks_per_seq = TT // C

    def blk(x, i):
        return x[i * GLA_SUB:(i + 1) * GLA_SUB]

    def rows16(r):
        return jnp.broadcast_to(r, (GLA_SUB, r.shape[1]))

    def stack(blocks):
        zero = jnp.zeros((GLA_SUB, GLA_DK), F32)
        return jnp.concatenate([zero if b is None else b for b in blocks], axis=0).astype(BF16)

    def segments(sub):
        if BB == 1:
            return [(0, sub * R, R)]
        return [(bi, 0, TT) for bi in range(BB)]

    def projection_stages(sub, z):
        g0 = (sub % SLOTS) * R
        rows = slice(g0, g0 + R)

        def norm():
            for idx, (bi, s0, n) in enumerate(segments(sub)):
                xb = x_ref[bi, s0:s0 + n, :]
                shift = mod_ref[0, bi, 0:1, :]
                scale = mod_ref[0, bi, 1:2, :]
                hmod = _rms(xb, pre_g * (1.0 + scale)) + shift
                hb_ref[g0 + idx * n:g0 + (idx + 1) * n, :] = hmod.astype(BF16)

        def proj(off, width):
            return _dot_nt(w_in_ref[0, off:off + width, :], hb_ref[rows, :]).T

        def keep(name, off, width):
            def f():
                z[name] = proj(off, width)
            return f

        def proj_a():
            a = proj(OFF_A, D_A)
            z["a"] = a
            for idx, (bi, s0, n) in enumerate(segments(sub)):
                ext_ref[bi, HIST_ROWS + s0:HIST_ROWS + s0 + n, :] = a[idx * n:(idx + 1) * n]

        def proj_q():
            q_ref[rows, :] = proj(OFF_Q, D_CK) * (GLA_DK ** -0.5)

        def proj_k():
            k_ref[rows, :] = proj(OFF_K, D_CK)

        def proj_v():
            v_ref[rows, :] = proj(OFF_VC, D_CV).astype(BF16)

        early = [norm, proj_a, keep("ga", OFF_GA, D_A), keep("gm0", OFF_GM, D_MODEL),
                 keep("vb", OFF_VB, D_B), keep("u", OFF_U, D_B), keep("gb", OFF_GB, D_B),
                 keep("gm1", OFF_GM + D_MODEL, D_MODEL), keep("lr", OFF_LR, LANE), proj_q, proj_k]
        last = [keep("gc", OFF_GC, D_CV), keep("gm2", OFF_GM + 2 * D_MODEL, D_MODEL)]
        return early, [proj_v], last

    def branch_stages(sub, z):
        g0 = (sub % SLOTS) * R
        rows = slice(g0, g0 + R)
        segs = segments(sub)
        c_first = g0 // C
        chunk_ids = list(range(c_first, c_first + R // C))
        w = {}

        def pool():
            parts = []
            for idx, (bi, s0, n) in enumerate(segs):
                ab = z["a"][idx * n:(idx + 1) * n]
                row = lax.broadcasted_iota(jnp.int32, (n, POOL_GW), 0)
                pos1 = row + (t * TT + s0 + pos0 + 1)
                groups = []
                for gi, win in enumerate(POOL_WINDOWS):
                    ls = slice(gi * POOL_GW, (gi + 1) * POOL_GW)
                    base = HIST_ROWS + s0
                    terms = [ext_ref[bi, base - k:base - k + n, ls] for k in range(win)]
                    while len(terms) > 1:
                        terms = [terms[i] + terms[i + 1] for i in range(0, len(terms), 2)]
                    cnt = jnp.minimum(pos1, win).astype(F32)
                    dgi = terms[0] / cnt - ab[:, ls]
                    groups.append(_dot(dgi.astype(BF16), pool_w_ref[0, gi].astype(BF16)))
                parts.append(jnp.concatenate(groups, axis=-1))
            w["ya"] = parts[0] if len(parts) == 1 else jnp.concatenate(parts, axis=0)

        def merge_a():
            y_a = w.pop("ya") * pool_scale_ref[0] * _silu(z.pop("ga"))
            w["merged"] = (_sigmoid(z.pop("gm0"))
                           * _dot(y_a.astype(BF16), w_oa_ref[0]))

        def sgu_norm():
            vb = z.pop("vb")
            mu = jnp.mean(vb, axis=-1, keepdims=True)
            vc = vb - mu
            vn = vc * lax.rsqrt(jnp.mean(vc * vc, axis=-1, keepdims=True) + EPS) * sgu_g_ref[0]
            if emit_vn:
                for idx, (bi, s0, n) in enumerate(segs):
                    vn_ref[bi, s0:s0 + n, :] = vn[idx * n:(idx + 1) * n]
            w["vnb"] = vn.astype(BF16)

        def sgu_mix():
            vnb = w.pop("vnb")
            s_rows = []
            for c0 in range(0, R, L):
                parts = [_dot(wms[g], vnb[c0:c0 + L, g * SGU_GW:(g + 1) * SGU_GW])
                         for g in range(SGU_GROUPS)]
                s_rows.append(jnp.concatenate(parts, axis=-1) + sgu_bias)
            s_all = s_rows[0] if len(s_rows) == 1 else jnp.concatenate(s_rows, axis=0)
            w["yb"] = z.pop("u") * s_all * _silu(z.pop("gb"))

        def merge_b():
            w["merged"] = w["merged"] + (_sigmoid(z.pop("gm1"))
                                         * _dot(w.pop("yb").astype(BF16), w_ob_ref[0]))

        def log_decay():
            pre = _dot(z.pop("lr").astype(BF16), wa2) + ba_ref[0]
            log_sig = jnp.minimum(pre, 0.0) - jnp.log(1.0 + jnp.exp2(jnp.abs(pre) * (-LOG2E)))
            la_ref[rows, :] = log_sig * (LOG2E / GLA_NORMALIZER)

        def cumulative(ci):
            def f():
                la = la_ref[ci * C:(ci + 1) * C, :]
                p0 = la.astype(BF16)
                p1 = (la - p0.astype(F32)).astype(BF16)
                w["b", ci] = _dot(tri, p0) + _dot(tri, p1)
            return f

        def operands(ci):
            def f():
                b_all = w.pop(("b", ci))
                for h in range(GLA_HEADS):
                    ks = slice(h * GLA_DK, (h + 1) * GLA_DK)
                    b = b_all[:, ks]
                    qh = q_ref[ci * C:(ci + 1) * C, ks]
                    kh = k_ref[ci * C:(ci + 1) * C, ks]
                    ends = [b[(i + 1) * GLA_SUB - 1:(i + 1) * GLA_SUB, :] for i in range(nb)]
                    starts = [None] + ends[:-1]
                    ref_start = jnp.concatenate([jnp.zeros((GLA_SUB, GLA_DK), F32)]
                                                + [rows16(e) for e in ends[:-1]], axis=0)
                    ref_end = jnp.concatenate([rows16(e) for e in ends], axis=0)
                    rel = b - ref_start
                    qd = qh * jnp.exp2(rel)
                    kd = kh * jnp.exp2(-rel)
                    ke = kh * jnp.exp2(ref_end - b)

                    def q_from(i, j):
                        if j == i - 1:
                            return blk(qd, i)
                        return blk(qd, i) * rows16(jnp.exp2(starts[i] - ends[j]))

                    def k_to(i, j):
                        if j == i:
                            return blk(ke, i)
                        return blk(ke, i) * rows16(jnp.exp2(ends[j] - ends[i]))

                    pairs = [(qd.astype(BF16), kd.astype(BF16))]
                    for (sz, _) in levels:
                        qblocks, kblocks = [], []
                        for i in range(nb):
                            mid = (i // (2 * sz)) * 2 * sz + sz
                            if (i // sz) % 2 == 1:
                                qblocks.append(q_from(i, mid - 1))
                                kblocks.append(None)
                            else:
                                qblocks.append(None)
                                kblocks.append(k_to(i, mid - 1))
                        pairs.append((stack(qblocks), stack(kblocks)))
                    q_in = stack([blk(qd, 0)]
                                 + [blk(qd, i) * rows16(jnp.exp2(starts[i])) for i in range(1, nb)])
                    k_dec = stack([k_to(i, nb - 1) for i in range(nb)])
                    w["ops", ci, h] = (q_in, k_dec, jnp.exp2(ends[nb - 1]), pairs)
            return f

        def attention(ci):
            def f():
                for h in range(GLA_HEADS):
                    q_in, k_dec, dec, pairs = w.pop(("ops", ci, h))
                    att = jnp.where(mask_diag, _dot_nt(*pairs[0]), 0.0)
                    for (_, same_blk), pr in zip(levels, pairs[1:]):
                        part = _dot_nt(*pr)
                        att = att + (part if same_blk is None else jnp.where(same_blk, part, 0.0))
                    w["att", ci, h] = (q_in, k_dec, dec, att.astype(BF16))
            return f

        def state_chain(ci):
            def f():
                bi = 0 if BB == 1 else ci // chunks_per_seq
                for h in range(GLA_HEADS):
                    q_in, k_dec, dec, att = w.pop(("att", ci, h))
                    vs = slice(h * GLA_DV, (h + 1) * GLA_DV)
                    vh = v_ref[ci * C:(ci + 1) * C, vs]
                    st = st_ref[bi, h]
                    o_ref[ci * C:(ci + 1) * C, vs] = _dot(att, vh) + _dot(q_in, st.astype(BF16))
                    dec_col = jnp.broadcast_to(dec, (GLA_DK, GLA_DK)).T
                    dec_col = jnp.concatenate([dec_col] * (GLA_DV // GLA_DK), axis=1)
                    st_ref[bi, h] = st * dec_col + _dot_tn(k_dec, vh)
            return f

        def merge_c():
            parts = [_rms(o_ref[rows, h * GLA_DV:(h + 1) * GLA_DV], gla_g) for h in range(GLA_HEADS)]
            y_c = jnp.concatenate(parts, axis=-1) * _silu(z.pop("gc"))
            w["merged"] = w["merged"] + (_sigmoid(z.pop("gm2"))
                                         * _dot(y_c.astype(BF16), w_oc_ref[0]))

        def out_proj():
            w["out"] = _dot(w.pop("merged").astype(BF16), w_out_ref[0])

        def residual():
            out = w.pop("out")
            for idx, (bi, s0, n) in enumerate(segs):
                gate = mod_ref[0, bi, 2:3, :]
                y_ref[bi, s0:s0 + n, :] = (x_ref[bi, s0:s0 + n, :]
                                           + _rms(out[idx * n:(idx + 1) * n], post_g * gate))

        early = [pool, merge_a, sgu_norm, sgu_mix, merge_b, log_decay]
        early += [cumulative(ci) for ci in chunk_ids]
        early += [operands(ci) for ci in chunk_ids]
        late = [attention(ci) for ci in chunk_ids]
        late += [state_chain(ci) for ci in chunk_ids]
        late += [merge_c, out_proj, residual]
        return early, late

    zs = [dict() for _ in range(NSUB)]
    proj = [projection_stages(sub, zs[sub]) for sub in range(NSUB)]
    branch = [branch_stages(sub, zs[sub]) for sub in range(NSUB)]
    _interleave(proj[0][0])
    for r in range(NSUB + 1):
        matmul_side = ((proj[r - 1][2] if r >= 1 else []) + (proj[r][1] if r < NSUB else [])
                       + (proj[r + 1][0] if r + 1 < NSUB else []))
        _interleave(branch[r - 1][1] if r >= 1 else [], branch[r][0] if r < NSUB else [], matmul_side)

    for bi in range(BB):
        tail = ext_ref[bi, TT:TT + HIST_ROWS, :]
        ext_ref[bi, 0:HIST_ROWS, :] = tail
        hist_ref[bi] = tail

    @pl.when(t == nt - 1)
    def _():
        sfin_ref[...] = st_ref[...]


def _layer_tiles(B, T):
    if T >= SUBTILES_PER_STEP * SUBTILE_ROWS:
        BB, TT, NSUB = 1, SUBTILES_PER_STEP * SUBTILE_ROWS, SUBTILES_PER_STEP
    else:
        BB, TT, NSUB = min(B, SUBTILE_ROWS // T), T, 1
    L = min(T, SGU_LEN)
    C = min(GLA_CHUNK, TT)
    R = BB * TT // NSUB
    assert T % TT == 0 and B % BB == 0 and TT >= HIST_ROWS
    assert (NSUB == 1 or BB == 1) and R % L == 0 and R % C == 0 and TT % C == 0
    return BB, TT, NSUB, L, C


def _vmem_limit(BB, TT, NSUB, weights, has_state):
    M = BB * TT
    ring = min(NSUB, SUBTILE_SLOTS) * (M // NSUB)
    wbytes = sum(math.prod(w.shape[1:]) * w.dtype.itemsize for w in weights)
    io = 2 * 2 * M * D_MODEL * 4
    state = 2 * (2 if has_state else 1) * BB * (GLA_HEADS * GLA_DK * GLA_DV + HIST_ROWS * D_A) * 4
    scratch = ring * (D_MODEL * 2 + 3 * D_CK * 4 + D_CV * 2 + D_CV * 4) + BB * (TT + HIST_ROWS) * D_A * 4 \
        + BB * GLA_HEADS * GLA_DK * GLA_DV * 4
    temps = ring * (3 * D_MODEL + 6 * D_MODEL) * 4
    return min(wbytes + io + state + scratch + temps, V7X_VMEM_BYTES - VMEM_RESERVE_BYTES)


def _mixer_layer(l, x, mod, mod_row0, state, weights, pos0, emit_vn):
    B, T, D = x.shape
    BB, TT, NSUB, L, C = _layer_tiles(B, T)
    ring = min(NSUB, SUBTILE_SLOTS) * (BB * TT // NSUB)
    has_state = state is not None
    mod_blk0 = mod_row0 // BB

    def wspec(w):
        nd = w.ndim
        return pl.BlockSpec((1,) + w.shape[1:], lambda b, t: (l,) + (0,) * (nd - 1),
                            pipeline_mode=pl.Buffered(1))

    grid = (B // BB, T // TT)
    inputs = [x, mod]
    in_specs = [
        pl.BlockSpec((BB, TT, D), lambda b, t: (b, t, 0)),
        pl.BlockSpec((1, BB, 3, D), lambda b, t: (l, mod_blk0 + b, 0, 0)),
    ]
    if has_state:
        inputs += list(state)
        in_specs += [
            pl.BlockSpec((1, BB, HIST_ROWS, D_A), lambda b, t: (l, b, 0, 0)),
            pl.BlockSpec((1, BB, GLA_HEADS, GLA_DK, GLA_DV), lambda b, t: (l, b, 0, 0, 0)),
        ]
    inputs += list(weights)
    in_specs += [wspec(w) for w in weights]
    out_shape = [
        jax.ShapeDtypeStruct((B, T, D), F32),
        jax.ShapeDtypeStruct((B, HIST_ROWS, D_A), F32),
        jax.ShapeDtypeStruct((B, GLA_HEADS, GLA_DK, GLA_DV), F32),
    ]
    out_specs = [
        pl.BlockSpec((BB, TT, D), lambda b, t: (b, t, 0)),
        pl.BlockSpec((BB, HIST_ROWS, D_A), lambda b, t: (b, 0, 0)),
        pl.BlockSpec((BB, GLA_HEADS, GLA_DK, GLA_DV), lambda b, t: (b, 0, 0, 0)),
    ]
    if emit_vn:
        out_shape.append(jax.ShapeDtypeStruct((B, T, D_B), F32))
        out_specs.append(pl.BlockSpec((BB, TT, D_B), lambda b, t: (b, t, 0)))
    scratch_shapes = [
        pltpu.VMEM((ring, D), BF16),
        pltpu.VMEM((BB, HIST_ROWS + TT, D_A), F32),
        pltpu.VMEM((ring, D_CK), F32),
        pltpu.VMEM((ring, D_CK), F32),
        pltpu.VMEM((ring, D_CV), BF16),
        pltpu.VMEM((ring, D_CK), F32),
        pltpu.VMEM((ring, D_CV), F32),
        pltpu.VMEM((BB, GLA_HEADS, GLA_DK, GLA_DV), F32),
    ]
    body = functools.partial(_layer_kernel, BB=BB, TT=TT, NSUB=NSUB, L=L, C=C, pos0=pos0,
                             has_state=has_state, emit_vn=emit_vn)
    return pl.pallas_call(
        body,
        grid=grid,
        in_specs=in_specs,
        out_specs=out_specs,
        out_shape=out_shape,
        scratch_shapes=scratch_shapes,
        compiler_params=pltpu.CompilerParams(
            dimension_semantics=("arbitrary", "arbitrary"),
            vmem_limit_bytes=_vmem_limit(BB, TT, NSUB, weights, has_state)),
        name="mixer_layer_T%d" % T,
    )(*inputs)


def kernel(x_prompt, x_sample, state_pool, state_gla, c_prompt, c_sample, ada_w, ada_b, pre_norm_g,
           post_norm_g, w_in, pool_w, pool_scale, sgu_norm_g, sgu_w, sgu_b, gla_wa2, gla_ba, gla_norm_g,
           w_oa, w_ob, w_oc, w_out):
    bp = x_prompt.shape[0]
    bs = x_sample.shape[0]
    mod = _ada_modulation(jnp.concatenate([c_prompt, c_sample], axis=0), ada_w, ada_b)
    mod = mod.reshape(DEPTH, bp + bs, 3, D_MODEL)

    w_pack = jnp.swapaxes(w_in, 1, 2).astype(BF16)
    sgu_bias = jnp.repeat(jnp.swapaxes(sgu_b, 1, 2), SGU_GW, axis=2)
    vec = lambda v: v[:, None, :]
    weights = [vec(pre_norm_g), vec(post_norm_g), w_pack, pool_w, vec(pool_scale), vec(sgu_norm_g),
               sgu_w, sgu_bias, gla_wa2, vec(gla_ba), vec(gla_norm_g),
               w_oa.astype(BF16), w_ob.astype(BF16), w_oc.astype(BF16), w_out.astype(BF16)]
    hist_sample = jnp.pad(state_pool, ((0, 0), (0, 0), (HIST_ROWS - POOL_HIST, 0), (0, 0)))

    xp, xs = x_prompt, x_sample
    pool_p, gla_p, pool_s, gla_s, sgu_s = [], [], [], [], []
    for l in range(DEPTH):
        xp, hp, sp = _mixer_layer(l, xp, mod, 0, None, weights, 0, False)
        xs, hs, ss, vs = _mixer_layer(l, xs, mod, bp, (hist_sample, state_gla), weights, PAST_LEN, True)
        pool_p.append(hp[:, HIST_ROWS - POOL_HIST:])
        gla_p.append(sp)
        pool_s.append(hs[:, HIST_ROWS - POOL_HIST:])
        gla_s.append(ss)
        sgu_s.append(vs)
    return (xp, xs, jnp.stack(pool_p), jnp.stack(gla_p), jnp.stack(pool_s), jnp.stack(gla_s),
            jnp.stack(sgu_s))
```

```python
import functools
import math

import jax
import jax.numpy as jnp
from jax import lax
from jax.experimental import pallas as pl
from jax.experimental.pallas import tpu as pltpu

D_MODEL = 1024
DEPTH = 2
PAST_LEN = 2048
EPS = 1e-6
POOL_WINDOWS = (2, 4, 8, 16)
POOL_GW = 128
D_A = 512
POOL_HIST = 15
HIST_ROWS = 16
SGU_LEN = 128
SGU_GROUPS = 4
D_B = 512
SGU_GW = 128
GLA_HEADS = 4
GLA_DK = 128
GLA_DV = 256
D_CK = 512
D_CV = 1024
GLA_RANK = 16
GLA_NORMALIZER = 16.0
GLA_SUB = 16
GLA_CHUNK = 128
SUBTILE_ROWS = 256
SUBTILES_PER_STEP = 2
SUBTILE_SLOTS = 2
ADA_COL_BLOCKS = 3
LANE = 128
V7X_VMEM_BYTES = 64 * 1024 * 1024
VMEM_RESERVE_BYTES = 4 * 1024 * 1024
LOG2E = math.log2(math.e)

OFF_A, OFF_GA, OFF_U, OFF_VB, OFF_GB = 0, 512, 1024, 1536, 2048
OFF_Q, OFF_K, OFF_VC, OFF_GC = 2560, 3072, 3584, 4608
OFF_LR = 5632
OFF_GM = OFF_LR + GLA_RANK

F32 = jnp.float32
BF16 = jnp.bfloat16


def _dot(a, b):
    return jnp.dot(a, b, preferred_element_type=F32)


def _dot_nt(a, b):
    return lax.dot_general(a, b, (((1,), (1,)), ((), ())), preferred_element_type=F32)


def _dot_tn(a, b):
    return lax.dot_general(a, b, (((0,), (0,)), ((), ())), preferred_element_type=F32)


def _sigmoid(x):
    return 1.0 / (1.0 + jnp.exp2(x * (-LOG2E)))


def _silu(x):
    return x * _sigmoid(x)


def _rms(x, g):
    return x * lax.rsqrt(jnp.mean(x * x, axis=-1, keepdims=True) + EPS) * g


def _ada_kernel(c_ref, w_ref, b_ref, o_ref):
    c = c_ref[...]
    o_ref[0] = _dot(_silu(c).astype(BF16), w_ref[0].astype(BF16)) + b_ref[0]


def _ada_modulation(c_all, ada_w, ada_b):
    n = c_all.shape[0]
    return pl.pallas_call(
        _ada_kernel,
        grid=(DEPTH, ADA_COL_BLOCKS),
        in_specs=[
            pl.BlockSpec((n, D_MODEL), lambda l, j: (0, 0)),
            pl.BlockSpec((1, D_MODEL, D_MODEL), lambda l, j: (l, 0, j)),
            pl.BlockSpec((1, 1, D_MODEL), lambda l, j: (l, 0, j)),
        ],
        out_specs=pl.BlockSpec((1, n, D_MODEL), lambda l, j: (l, 0, j)),
        out_shape=jax.ShapeDtypeStruct((DEPTH, n, 3 * D_MODEL), F32),
        name="ada_modulation",
    )(c_all, ada_w, ada_b.reshape(DEPTH, 1, 3 * D_MODEL))


def _interleave(*stage_lists):
    lists = [sl for sl in stage_lists if sl]
    done = [0] * len(lists)
    while any(d < len(sl) for d, sl in zip(done, lists)):
        k = min((i for i in range(len(lists)) if done[i] < len(lists[i])),
                key=lambda i: done[i] / len(lists[i]))
        lists[k][done[k]]()
        done[k] += 1


def _layer_kernel(*refs, BB, TT, NSUB, L, C, pos0, has_state, emit_vn):
    refs = list(refs)
    x_ref, mod_ref = refs[:2]
    del refs[:2]
    if has_state:
        hist0_ref, s0_ref = refs[:2]
        del refs[:2]
    (pre_g_ref, post_g_ref, w_in_ref, pool_w_ref, pool_scale_ref, sgu_g_ref, sgu_w_ref, sgu_b_ref,
     wa2_ref, ba_ref, gla_g_ref, w_oa_ref, w_ob_ref, w_oc_ref, w_out_ref) = refs[:15]
    del refs[:15]
    y_ref, hist_ref, sfin_ref = refs[:3]
    del refs[:3]
    if emit_vn:
        vn_ref = refs.pop(0)
    hb_ref, ext_ref, q_ref, k_ref, v_ref, la_ref, o_ref, st_ref = refs

    M = BB * TT
    R = M // NSUB
    SLOTS = min(NSUB, SUBTILE_SLOTS)
    t = pl.program_id(1)
    nt = pl.num_programs(1)

    @pl.when(t == 0)
    def _():
        if has_state:
            ext_ref[:, 0:HIST_ROWS, :] = hist0_ref[0]
            st_ref[...] = s0_ref[0]
        else:
            ext_ref[:, 0:HIST_ROWS, :] = jnp.zeros((BB, HIST_ROWS, D_A), F32)
            st_ref[...] = jnp.zeros(st_ref.shape, F32)

    pre_g = pre_g_ref[0]
    post_g = post_g_ref[0]
    gla_g = gla_g_ref[0]
    ri = lax.broadcasted_iota(jnp.int32, (L, L), 0)
    ci_ = lax.broadcasted_iota(jnp.int32, (L, L), 1)
    wms = [jnp.where(ri >= ci_, sgu_w_ref[0, g, 0:L, 0:L], 0.0).astype(BF16) for g in range(SGU_GROUPS)]
    sgu_bias = sgu_b_ref[0, 0:L, :]
    wa2 = jnp.concatenate([wa2_ref[0], jnp.zeros((LANE - GLA_RANK, D_CK), F32)], axis=0).astype(BF16)
    cr = lax.broadcasted_iota(jnp.int32, (C, C), 0)
    cc = lax.broadcasted_iota(jnp.int32, (C, C), 1)
    tri = (cr >= cc).astype(BF16)
    sub_shift = GLA_SUB.bit_length() - 1
    mask_diag = (cr >= cc) & ((cr >> sub_shift) == (cc >> sub_shift))
    nb = C // GLA_SUB
    levels = []
    s = 1
    while 2 * s <= nb:
        sh = sub_shift + s.bit_length()
        levels.append((s, None if 2 * s == nb else (cr >> sh) == (cc >> sh)))
        s *= 2
    chunks_per_seq = TT // C

    def blk(x, i):
        return x[i * GLA_SUB:(i + 1) * GLA_SUB]

    def rows16(r):
        return jnp.broadcast_to(r, (GLA_SUB, r.shape[1]))

    def stack(blocks):
        zero = jnp.zeros((GLA_SUB, GLA_DK), F32)
        return jnp.concatenate([zero if b is None else b for b in blocks], axis=0).astype(BF16)

    def segments(sub):
        if BB == 1:
            return [(0, sub * R, R)]
        return [(bi, 0, TT) for bi in range(BB)]

    def projection_stages(sub, z):
        g0 = (sub % SLOTS) * R
        rows = slice(g0, g0 + R)

        def norm():
            for idx, (bi, s0, n) in enumerate(segments(sub)):
                xb = x_ref[bi, s0:s0 + n, :]
                shift = mod_ref[0, bi, 0:1, :]
                scale = mod_ref[0, bi, 1:2, :]
                hmod = _rms(xb, pre_g * (1.0 + scale)) + shift
                hb_ref[g0 + idx * n:g0 + (idx + 1) * n, :] = hmod.astype(BF16)

        def proj(off, width):
            return _dot_nt(w_in_ref[0, off:off + width, :], hb_ref[rows, :]).T

        def keep(name, off, width):
            def f():
                z[name] = proj(off, width)
            return f

        def proj_a():
            a = proj(OFF_A, D_A)
            z["a"] = a
            for idx, (bi, s0, n) in enumerate(segments(sub)):
                ext_ref[bi, HIST_ROWS + s0:HIST_ROWS + s0 + n, :] = a[idx * n:(idx + 1) * n]

        def proj_q():
            q_ref[rows, :] = proj(OFF_Q, D_CK) * (GLA_DK ** -0.5)

        def proj_k():
            k_ref[rows, :] = proj(OFF_K, D_CK)

        def proj_v():
            v_ref[rows, :] = proj(OFF_VC, D_CV).astype(BF16)

        early = [norm, proj_a, keep("ga", OFF_GA, D_A), keep("gm0", OFF_GM, D_MODEL),
                 keep("vb", OFF_VB, D_B), keep("u", OFF_U, D_B), keep("gb", OFF_GB, D_B),
                 keep("gm1", OFF_GM + D_MODEL, D_MODEL), keep("lr", OFF_LR, LANE), proj_q, proj_k]
        last = [keep("gc", OFF_GC, D_CV), keep("gm2", OFF_GM + 2 * D_MODEL, D_MODEL)]
        return early, [proj_v], last

    def branch_stages(sub, z):
        g0 = (sub % SLOTS) * R
        rows = slice(g0, g0 + R)
        segs = segments(sub)
        c_first = g0 // C
        chunk_ids = list(range(c_first, c_first + R // C))
        w = {}

        def pool():
            parts = []
            for idx, (bi, s0, n) in enumerate(segs):
                ab = z["a"][idx * n:(idx + 1) * n]
                row = lax.broadcasted_iota(jnp.int32, (n, POOL_GW), 0)
                pos1 = row + (t * TT + s0 + pos0 + 1)
                groups = []
                for gi, win in enumerate(POOL_WINDOWS):
                    ls = slice(gi * POOL_GW, (gi + 1) * POOL_GW)
                    base = HIST_ROWS + s0
                    terms = [ext_ref[bi, base - k:base - k + n, ls] for k in range(win)]
                    while len(terms) > 1:
                        terms = [terms[i] + terms[i + 1] for i in range(0, len(terms), 2)]
                    cnt = jnp.minimum(pos1, win).astype(F32)
                    dgi = terms[0] / cnt - ab[:, ls]
                    groups.append(_dot(dgi.astype(BF16), pool_w_ref[0, gi].astype(BF16)))
                parts.append(jnp.concatenate(groups, axis=-1))
            w["ya"] = parts[0] if len(parts) == 1 else jnp.concatenate(parts, axis=0)

        def merge_a():
            y_a = w.pop("ya") * pool_scale_ref[0] * _silu(z.pop("ga"))
            w["merged"] = (_sigmoid(z.pop("gm0"))
                           * _dot_nt(w_oa_ref[0], y_a.astype(BF16)).T)

        def sgu_norm():
            vb = z.pop("vb")
            mu = jnp.mean(vb, axis=-1, keepdims=True)
            vc = vb - mu
            vn = vc * lax.rsqrt(jnp.mean(vc * vc, axis=-1, keepdims=True) + EPS) * sgu_g_ref[0]
            if emit_vn:
                for idx, (bi, s0, n) in enumerate(segs):
                    vn_ref[bi, s0:s0 + n, :] = vn[idx * n:(idx + 1) * n]
            w["vnb"] = vn.astype(BF16)

        def sgu_mix():
            vnb = w.pop("vnb")
            s_rows = []
            for c0 in range(0, R, L):
                parts = [_dot(wms[g], vnb[c0:c0 + L, g * SGU_GW:(g + 1) * SGU_GW])
                         for g in range(SGU_GROUPS)]
                s_rows.append(jnp.concatenate(parts, axis=-1) + sgu_bias)
            s_all = s_rows[0] if len(s_rows) == 1 else jnp.concatenate(s_rows, axis=0)
            w["yb"] = z.pop("u") * s_all * _silu(z.pop("gb"))

        def merge_b():
            w["merged"] = w["merged"] + (_sigmoid(z.pop("gm1"))
                                         * _dot_nt(w_ob_ref[0], w.pop("yb").astype(BF16)).T)

        def log_decay():
            pre = _dot(z.pop("lr").astype(BF16), wa2) + ba_ref[0]
            log_sig = jnp.minimum(pre, 0.0) - jnp.log(1.0 + jnp.exp2(jnp.abs(pre) * (-LOG2E)))
            la_ref[rows, :] = log_sig * (LOG2E / GLA_NORMALIZER)

        def cumulative(ci):
            def f():
                la = la_ref[ci * C:(ci + 1) * C, :]
                p0 = la.astype(BF16)
                p1 = (la - p0.astype(F32)).astype(BF16)
                w["b", ci] = _dot(tri, p0) + _dot(tri, p1)
            return f

        def operands(ci):
            def f():
                b_all = w.pop(("b", ci))
                for h in range(GLA_HEADS):
                    ks = slice(h * GLA_DK, (h + 1) * GLA_DK)
                    b = b_all[:, ks]
                    qh = q_ref[ci * C:(ci + 1) * C, ks]
                    kh = k_ref[ci * C:(ci + 1) * C, ks]
                    ends = [b[(i + 1) * GLA_SUB - 1:(i + 1) * GLA_SUB, :] for i in range(nb)]
                    starts = [None] + ends[:-1]
                    ref_start = jnp.concatenate([jnp.zeros((GLA_SUB, GLA_DK), F32)]
                                                + [rows16(e) for e in ends[:-1]], axis=0)
                    ref_end = jnp.concatenate([rows16(e) for e in ends], axis=0)
                    rel = b - ref_start
                    qd = qh * jnp.exp2(rel)
                    kd = kh * jnp.exp2(-rel)
                    ke = kh * jnp.exp2(ref_end - b)

                    def q_from(i, j):
                        if j == i - 1:
                            return blk(qd, i)
                        return blk(qd, i) * rows16(jnp.exp2(starts[i] - ends[j]))

                    def k_to(i, j):
                        if j == i:
                            return blk(ke, i)
                        return blk(ke, i) * rows16(jnp.exp2(ends[j] - ends[i]))

                    pairs = [(qd.astype(BF16), kd.astype(BF16))]
                    for (sz, _) in levels:
                        qblocks, kblocks = [], []
                        for i in range(nb):
                            mid = (i // (2 * sz)) * 2 * sz + sz
                            if (i // sz) % 2 == 1:
                                qblocks.append(q_from(i, mid - 1))
                                kblocks.append(None)
                            else:
                                qblocks.append(None)
                                kblocks.append(k_to(i, mid - 1))
                        pairs.append((stack(qblocks), stack(kblocks)))
                    q_in = stack([blk(qd, 0)]
                                 + [blk(qd, i) * rows16(jnp.exp2(starts[i])) for i in range(1, nb)])
                    k_dec = stack([k_to(i, nb - 1) for i in range(nb)])
                    w["ops", ci, h] = (q_in, k_dec, jnp.exp2(ends[nb - 1]), pairs)
            return f

        def attention(ci):
            def f():
                for h in range(GLA_HEADS):
                    q_in, k_dec, dec, pairs = w.pop(("ops", ci, h))
                    att = jnp.where(mask_diag, _dot_nt(*pairs[0]), 0.0)
                    for (_, same_blk), pr in zip(levels, pairs[1:]):
                        part = _dot_nt(*pr)
                        att = att + (part if same_blk is None else jnp.where(same_blk, part, 0.0))
                    w["att", ci, h] = (q_in, k_dec, dec, att.astype(BF16))
            return f

        def state_chain(ci):
            def f():
                bi = 0 if BB == 1 else ci // chunks_per_seq
                for h in range(GLA_HEADS):
                    q_in, k_dec, dec, att = w.pop(("att", ci, h))
                    vs = slice(h * GLA_DV, (h + 1) * GLA_DV)
                    vh = v_ref[ci * C:(ci + 1) * C, vs]
                    st = st_ref[bi, h]
                    o_ref[ci * C:(ci + 1) * C, vs] = _dot(att, vh) + _dot(q_in, st.astype(BF16))
                    dec_col = jnp.broadcast_to(dec, (GLA_DK, GLA_DK)).T
                    dec_col = jnp.concatenate([dec_col] * (GLA_DV // GLA_DK), axis=1)
                    st_ref[bi, h] = st * dec_col + _dot_tn(k_dec, vh)
            return f

        def merge_c():
            parts = [_rms(o_ref[rows, h * GLA_DV:(h + 1) * GLA_DV], gla_g) for h in range(GLA_HEADS)]
            y_c = jnp.concatenate(parts, axis=-1) * _silu(z.pop("gc"))
            w["merged"] = w["merged"] + (_sigmoid(z.pop("gm2"))
                                         * _dot_nt(w_oc_ref[0], y_c.astype(BF16)).T)

        def out_proj():
            w["out"] = _dot_nt(w_out_ref[0], w.pop("merged").astype(BF16)).T

        def residual():
            out = w.pop("out")
            for idx, (bi, s0, n) in enumerate(segs):
                gate = mod_ref[0, bi, 2:3, :]
                y_ref[bi, s0:s0 + n, :] = (x_ref[bi, s0:s0 + n, :]
                                           + _rms(out[idx * n:(idx + 1) * n], post_g * gate))

        early = [pool, merge_a, sgu_norm, sgu_mix, merge_b, log_decay]
        early += [cumulative(ci) for ci in chunk_ids]
        early += [operands(ci) for ci in chunk_ids]
        late = [attention(ci) for ci in chunk_ids]
        late += [state_chain(ci) for ci in chunk_ids]
        late += [merge_c, out_proj, residual]
        return early, late

    zs = [dict() for _ in range(NSUB)]
    proj = [projection_stages(sub, zs[sub]) for sub in range(NSUB)]
    branch = [branch_stages(sub, zs[sub]) for sub in range(NSUB)]
    _interleave(proj[0][0])
    for r in range(NSUB + 1):
        matmul_side = ((proj[r - 1][2] if r >= 1 else []) + (proj[r][1] if r < NSUB else [])
                       + (proj[r + 1][0] if r + 1 < NSUB else []))
        _interleave(branch[r - 1][1] if r >= 1 else [], branch[r][0] if r < NSUB else [], matmul_side)

    for bi in range(BB):
        tail = ext_ref[bi, TT:TT + HIST_ROWS, :]
        ext_ref[bi, 0:HIST_ROWS, :] = tail
        hist_ref[bi] = tail

    @pl.when(t == nt - 1)
    def _():
        sfin_ref[...] = st_ref[...]


def _layer_tiles(B, T):
    if T >= SUBTILES_PER_STEP * SUBTILE_ROWS:
        BB, TT, NSUB = 1, SUBTILES_PER_STEP * SUBTILE_ROWS, SUBTILES_PER_STEP
    else:
        BB, TT, NSUB = min(B, SUBTILE_ROWS // T), T, 1
    L = min(T, SGU_LEN)
    C = min(GLA_CHUNK, TT)
    R = BB * TT // NSUB
    assert T % TT == 0 and B % BB == 0 and TT >= HIST_ROWS
    assert (NSUB == 1 or BB == 1) and R % L == 0 and R % C == 0 and TT % C == 0
    return BB, TT, NSUB, L, C


def _vmem_limit(BB, TT, NSUB, weights, has_state):
    M = BB * TT
    ring = min(NSUB, SUBTILE_SLOTS) * (M // NSUB)
    wbytes = sum(math.prod(w.shape[1:]) * w.dtype.itemsize for w in weights)
    io = 2 * 2 * M * D_MODEL * 4
    state = 2 * (2 if has_state else 1) * BB * (GLA_HEADS * GLA_DK * GLA_DV + HIST_ROWS * D_A) * 4
    scratch = ring * (D_MODEL * 2 + 3 * D_CK * 4 + D_CV * 2 + D_CV * 4) + BB * (TT + HIST_ROWS) * D_A * 4 \
        + BB * GLA_HEADS * GLA_DK * GLA_DV * 4
    temps = ring * (3 * D_MODEL + 6 * D_MODEL) * 4
    return min(wbytes + io + state + scratch + temps, V7X_VMEM_BYTES - VMEM_RESERVE_BYTES)


def _mixer_layer(l, x, mod, mod_row0, state, weights, pos0, emit_vn):
    B, T, D = x.shape
    BB, TT, NSUB, L, C = _layer_tiles(B, T)
    ring = min(NSUB, SUBTILE_SLOTS) * (BB * TT // NSUB)
    has_state = state is not None
    mod_blk0 = mod_row0 // BB

    def wspec(w):
        nd = w.ndim
        return pl.BlockSpec((1,) + w.shape[1:], lambda b, t: (l,) + (0,) * (nd - 1),
                            pipeline_mode=pl.Buffered(1))

    grid = (B // BB, T // TT)
    inputs = [x, mod]
    in_specs = [
        pl.BlockSpec((BB, TT, D), lambda b, t: (b, t, 0)),
        pl.BlockSpec((1, BB, 3, D), lambda b, t: (l, mod_blk0 + b, 0, 0)),
    ]
    if has_state:
        inputs += list(state)
        in_specs += [
            pl.BlockSpec((1, BB, HIST_ROWS, D_A), lambda b, t: (l, b, 0, 0)),
            pl.BlockSpec((1, BB, GLA_HEADS, GLA_DK, GLA_DV), lambda b, t: (l, b, 0, 0, 0)),
        ]
    inputs += list(weights)
    in_specs += [wspec(w) for w in weights]
    out_shape = [
        jax.ShapeDtypeStruct((B, T, D), F32),
        jax.ShapeDtypeStruct((B, HIST_ROWS, D_A), F32),
        jax.ShapeDtypeStruct((B, GLA_HEADS, GLA_DK, GLA_DV), F32),
    ]
    out_specs = [
        pl.BlockSpec((BB, TT, D), lambda b, t: (b, t, 0)),
        pl.BlockSpec((BB, HIST_ROWS, D_A), lambda b, t: (b, 0, 0)),
        pl.BlockSpec((BB, GLA_HEADS, GLA_DK, GLA_DV), lambda b, t: (b, 0, 0, 0)),
    ]
    if emit_vn:
        out_shape.append(jax.ShapeDtypeStruct((B, T, D_B), F32))
        out_specs.append(pl.BlockSpec((BB, TT, D_B), lambda b, t: (b, t, 0)))
    scratch_shapes = [
        pltpu.VMEM((ring, D), BF16),
        pltpu.VMEM((BB, HIST_ROWS + TT, D_A), F32),
        pltpu.VMEM((ring, D_CK), F32),
        pltpu.VMEM((ring, D_CK), F32),
        pltpu.VMEM((ring, D_CV), BF16),
        pltpu.VMEM((ring, D_CK), F32),
        pltpu.VMEM((ring, D_CV), F32),
        pltpu.VMEM((BB, GLA_HEADS, GLA_DK, GLA_DV), F32),
    ]
    body = functools.partial(_layer_kernel, BB=BB, TT=TT, NSUB=NSUB, L=L, C=C, pos0=pos0,
                             has_state=has_state, emit_vn=emit_vn)
    return pl.pallas_call(
        body,
        grid=grid,
        in_specs=in_specs,
        out_specs=out_specs,
        out_shape=out_shape,
        scratch_shapes=scratch_shapes,
        compiler_params=pltpu.CompilerParams(
            dimension_semantics=("arbitrary", "arbitrary"),
            vmem_limit_bytes=_vmem_limit(BB, TT, NSUB, weights, has_state)),
        name="mixer_layer_T%d" % T,
    )(*inputs)


def kernel(x_prompt, x_sample, state_pool, state_gla, c_prompt, c_sample, ada_w, ada_b, pre_norm_g,
           post_norm_g, w_in, pool_w, pool_scale, sgu_norm_g, sgu_w, sgu_b, gla_wa2, gla_ba, gla_norm_g,
           w_oa, w_ob, w_oc, w_out):
    bp = x_prompt.shape[0]
    bs = x_sample.shape[0]
    mod = _ada_modulation(jnp.concatenate([c_prompt, c_sample], axis=0), ada_w, ada_b)
    mod = mod.reshape(DEPTH, bp + bs, 3, D_MODEL)

    w_pack = jnp.swapaxes(w_in, 1, 2).astype(BF16)
    sgu_bias = jnp.repeat(jnp.swapaxes(sgu_b, 1, 2), SGU_GW, axis=2)
    vec = lambda v: v[:, None, :]
    weights = [vec(pre_norm_g), vec(post_norm_g), w_pack, pool_w, vec(pool_scale), vec(sgu_norm_g),
               sgu_w, sgu_bias, gla_wa2, vec(gla_ba), vec(gla_norm_g),
               *(jnp.swapaxes(wo, 1, 2).astype(BF16) for wo in (w_oa, w_ob, w_oc, w_out))]
    hist_sample = jnp.pad(state_pool, ((0, 0), (0, 0), (HIST_ROWS - POOL_HIST, 0), (0, 0)))

    xp, xs = x_prompt, x_sample
    pool_p, gla_p, pool_s, gla_s, sgu_s = [], [], [], [], []
    for l in range(DEPTH):
        xp, hp, sp = _mixer_layer(l, xp, mod, 0, None, weights, 0, False)
        xs, hs, ss, vs = _mixer_layer(l, xs, mod, bp, (hist_sample, state_gla), weights, PAST_LEN, True)
        pool_p.append(hp[:, HIST_ROWS - POOL_HIST:])
        gla_p.append(sp)
        pool_s.append(hs[:, HIST_ROWS - POOL_HIST:])
        gla_s.append(ss)
        sgu_s.append(vs)
    return (xp, xs, jnp.stack(pool_p), jnp.stack(gla_p), jnp.stack(pool_s), jnp.stack(gla_s),
            jnp.stack(sgu_s))
```

```python
import functools
import math

import jax
import jax.numpy as jnp
from jax import lax
from jax.experimental import pallas as pl
from jax.experimental.pallas import tpu as pltpu

D_MODEL = 1024
DEPTH = 2
PAST_LEN = 2048
EPS = 1e-6
POOL_WINDOWS = (2, 4, 8, 16)
POOL_GW = 128
D_A = 512
POOL_HIST = 15
HIST_ROWS = 16
SGU_LEN = 128
SGU_GROUPS = 4
D_B = 512
SGU_GW = 128
GLA_HEADS = 4
GLA_DK = 128
GLA_DV = 256
D_CK = 512
D_CV = 1024
GLA_RANK = 16
GLA_NORMALIZER = 16.0
GLA_SUB = 16
GLA_CHUNK = 128
SUBTILE_ROWS = 256
SUBTILES_PER_STEP = 2
SUBTILE_SLOTS = 2
ADA_COL_BLOCKS = 3
LANE = 128
V7X_VMEM_BYTES = 64 * 1024 * 1024
VMEM_RESERVE_BYTES = 4 * 1024 * 1024
LOG2E = math.log2(math.e)

OFF_A, OFF_GA, OFF_U, OFF_VB, OFF_GB = 0, 512, 1024, 1536, 2048
OFF_Q, OFF_K, OFF_VC, OFF_GC = 2560, 3072, 3584, 4608
OFF_LR = 5632
OFF_GM = OFF_LR + GLA_RANK

F32 = jnp.float32
BF16 = jnp.bfloat16


def _dot(a, b):
    return jnp.dot(a, b, preferred_element_type=F32)


def _dot_nt(a, b):
    return lax.dot_general(a, b, (((1,), (1,)), ((), ())), preferred_element_type=F32)


def _dot_tn(a, b):
    return lax.dot_general(a, b, (((0,), (0,)), ((), ())), preferred_element_type=F32)


def _sigmoid(x):
    return 1.0 / (1.0 + jnp.exp2(x * (-LOG2E)))


def _silu(x):
    return x * _sigmoid(x)


def _rms(x, g):
    return x * lax.rsqrt(jnp.mean(x * x, axis=-1, keepdims=True) + EPS) * g


def _ada_kernel(c_ref, w_ref, b_ref, o_ref):
    c = c_ref[...]
    o_ref[0] = _dot(_silu(c).astype(BF16), w_ref[0].astype(BF16)) + b_ref[0]


def _ada_modulation(c_all, ada_w, ada_b):
    n = c_all.shape[0]
    return pl.pallas_call(
        _ada_kernel,
        grid=(DEPTH, ADA_COL_BLOCKS),
        in_specs=[
            pl.BlockSpec((n, D_MODEL), lambda l, j: (0, 0)),
            pl.BlockSpec((1, D_MODEL, D_MODEL), lambda l, j: (l, 0, j)),
            pl.BlockSpec((1, 1, D_MODEL), lambda l, j: (l, 0, j)),
        ],
        out_specs=pl.BlockSpec((1, n, D_MODEL), lambda l, j: (l, 0, j)),
        out_shape=jax.ShapeDtypeStruct((DEPTH, n, 3 * D_MODEL), F32),
        name="ada_modulation",
    )(c_all, ada_w, ada_b.reshape(DEPTH, 1, 3 * D_MODEL))


def _interleave(*stage_lists):
    lists = [sl for sl in stage_lists if sl]
    done = [0] * len(lists)
    while any(d < len(sl) for d, sl in zip(done, lists)):
        k = min((i for i in range(len(lists)) if done[i] < len(lists[i])),
                key=lambda i: done[i] / len(lists[i]))
        lists[k][done[k]]()
        done[k] += 1


def _layer_kernel(*refs, BB, TT, NSUB, L, C, pos0, has_state, emit_vn):
    refs = list(refs)
    x_ref, mod_ref = refs[:2]
    del refs[:2]
    if has_state:
        hist0_ref, s0_ref = refs[:2]
        del refs[:2]
    (pre_g_ref, post_g_ref, w_in_ref, pool_w_ref, pool_scale_ref, sgu_g_ref, sgu_w_ref, sgu_b_ref,
     wa2_ref, ba_ref, gla_g_ref, w_oa_ref, w_ob_ref, w_oc_ref, w_out_ref) = refs[:15]
    del refs[:15]
    y_ref, hist_ref, sfin_ref = refs[:3]
    del refs[:3]
    if emit_vn:
        vn_ref = refs.pop(0)
    hb_ref, ext_ref, q_ref, k_ref, v_ref, la_ref, o_ref, st_ref = refs

    M = BB * TT
    R = M // NSUB
    SLOTS = min(NSUB, SUBTILE_SLOTS)
    t = pl.program_id(1)
    nt = pl.num_programs(1)

    @pl.when(t == 0)
    def _():
        if has_state:
            ext_ref[:, 0:HIST_ROWS, :] = hist0_ref[0]
            st_ref[...] = s0_ref[0]
        else:
            ext_ref[:, 0:HIST_ROWS, :] = jnp.zeros((BB, HIST_ROWS, D_A), F32)
            st_ref[...] = jnp.zeros(st_ref.shape, F32)

    pre_g = pre_g_ref[0]
    post_g = post_g_ref[0]
    gla_g = gla_g_ref[0]
    ri = lax.broadcasted_iota(jnp.int32, (L, L), 0)
    ci_ = lax.broadcasted_iota(jnp.int32, (L, L), 1)
    wms = [jnp.where(ri >= ci_, sgu_w_ref[0, g, 0:L, 0:L], 0.0).astype(BF16) for g in range(SGU_GROUPS)]
    sgu_bias = sgu_b_ref[0, 0:L, :]
    wa2 = jnp.concatenate([wa2_ref[0], jnp.zeros((LANE - GLA_RANK, D_CK), F32)], axis=0).astype(BF16)
    cr = lax.broadcasted_iota(jnp.int32, (C, C), 0)
    cc = lax.broadcasted_iota(jnp.int32, (C, C), 1)
    tri = (cr >= cc).astype(BF16)
    sub_shift = GLA_SUB.bit_length() - 1
    mask_diag = (cr >= cc) & ((cr >> sub_shift) == (cc >> sub_shift))
    nb = C // GLA_SUB
    levels = []
    s = 1
    while 2 * s <= nb:
        sh = sub_shift + s.bit_length()
        levels.append((s, None if 2 * s == nb else (cr >> sh) == (cc >> sh)))
        s *= 2
    chunks_per_seq = TT // C

    def blk(x, i):
        return x[i * GLA_SUB:(i + 1) * GLA_SUB]

    def rows16(r):
        return jnp.broadcast_to(r, (GLA_SUB, r.shape[1]))

    def stack(blocks):
        zero = jnp.zeros((GLA_SUB, GLA_DK), F32)
        return jnp.concatenate([zero if b is None else b for b in blocks], axis=0).astype(BF16)

    def segments(sub):
        if BB == 1:
            return [(0, sub * R, R)]
        return [(bi, 0, TT) for bi in range(BB)]

    def projection_stages(sub, z):
        g0 = (sub % SLOTS) * R
        rows = slice(g0, g0 + R)

        def norm():
            for idx, (bi, s0, n) in enumerate(segments(sub)):
                xb = x_ref[bi, s0:s0 + n, :]
                shift = mod_ref[0, bi, 0:1, :]
                scale = mod_ref[0, bi, 1:2, :]
                hmod = _rms(xb, pre_g * (1.0 + scale)) + shift
                hb_ref[g0 + idx * n:g0 + (idx + 1) * n, :] = hmod.astype(BF16)

        def proj(off, width):
            return _dot_nt(w_in_ref[0, off:off + width, :], hb_ref[rows, :]).T

        def keep(name, off, width):
            def f():
                z[name] = proj(off, width)
            return f

        def put_a(a):
            z["a"] = a
            for idx, (bi, s0, n) in enumerate(segments(sub)):
                ext_ref[bi, HIST_ROWS + s0:HIST_ROWS + s0 + n, :] = a[idx * n:(idx + 1) * n]

        def put_q(q):
            q_ref[rows, :] = q * (GLA_DK ** -0.5)

        def put_k(k):
            k_ref[rows, :] = k

        def put(name):
            return lambda val: z.__setitem__(name, val)

        def proj_v():
            v_ref[rows, :] = proj(OFF_VC, D_CV).astype(BF16)

        def joint(pieces):
            lo = pieces[0][1]
            hi = max(off + width for _, off, width in pieces)

            def f():
                zt = _dot_nt(w_in_ref[0, lo:hi, :], hb_ref[rows, :])
                for consume, off, width in pieces:
                    consume(zt[off - lo:off - lo + width, :].T)
            return f

        early = [norm,
                 joint([(put_a, OFF_A, D_A), (put("ga"), OFF_GA, D_A), (put("u"), OFF_U, D_B),
                        (put("vb"), OFF_VB, D_B), (put("gb"), OFF_GB, D_B), (put_q, OFF_Q, D_CK),
                        (put_k, OFF_K, D_CK)]),
                 joint([(put("lr"), OFF_LR, LANE), (put("gm0"), OFF_GM, D_MODEL),
                        (put("gm1"), OFF_GM + D_MODEL, D_MODEL)])]
        last = [keep("gc", OFF_GC, D_CV), keep("gm2", OFF_GM + 2 * D_MODEL, D_MODEL)]
        return early, [proj_v], last

    def branch_stages(sub, z):
        g0 = (sub % SLOTS) * R
        rows = slice(g0, g0 + R)
        segs = segments(sub)
        c_first = g0 // C
        chunk_ids = list(range(c_first, c_first + R // C))
        w = {}

        def pool():
            parts = []
            for idx, (bi, s0, n) in enumerate(segs):
                ab = z["a"][idx * n:(idx + 1) * n]
                row = lax.broadcasted_iota(jnp.int32, (n, POOL_GW), 0)
                pos1 = row + (t * TT + s0 + pos0 + 1)
                groups = []
                for gi, win in enumerate(POOL_WINDOWS):
                    ls = slice(gi * POOL_GW, (gi + 1) * POOL_GW)
                    base = HIST_ROWS + s0
                    terms = [ext_ref[bi, base - k:base - k + n, ls] for k in range(win)]
                    while len(terms) > 1:
                        terms = [terms[i] + terms[i + 1] for i in range(0, len(terms), 2)]
                    cnt = jnp.minimum(pos1, win).astype(F32)
                    dgi = terms[0] / cnt - ab[:, ls]
                    groups.append(_dot(dgi.astype(BF16), pool_w_ref[0, gi].astype(BF16)))
                parts.append(jnp.concatenate(groups, axis=-1))
            w["ya"] = parts[0] if len(parts) == 1 else jnp.concatenate(parts, axis=0)

        def merge_a():
            y_a = w.pop("ya") * pool_scale_ref[0] * _silu(z.pop("ga"))
            w["merged"] = (_sigmoid(z.pop("gm0"))
                           * _dot(y_a.astype(BF16), w_oa_ref[0]))

        def sgu_norm():
            vb = z.pop("vb")
            mu = jnp.mean(vb, axis=-1, keepdims=True)
            vc = vb - mu
            vn = vc * lax.rsqrt(jnp.mean(vc * vc, axis=-1, keepdims=True) + EPS) * sgu_g_ref[0]
            if emit_vn:
                for idx, (bi, s0, n) in enumerate(segs):
                    vn_ref[bi, s0:s0 + n, :] = vn[idx * n:(idx + 1) * n]
            w["vnb"] = vn.astype(BF16)

        def sgu_mix():
            vnb = w.pop("vnb")
            s_rows = []
            for c0 in range(0, R, L):
                parts = [_dot(wms[g], vnb[c0:c0 + L, g * SGU_GW:(g + 1) * SGU_GW])
                         for g in range(SGU_GROUPS)]
                s_rows.append(jnp.concatenate(parts, axis=-1) + sgu_bias)
            s_all = s_rows[0] if len(s_rows) == 1 else jnp.concatenate(s_rows, axis=0)
            w["yb"] = z.pop("u") * s_all * _silu(z.pop("gb"))

        def merge_b():
            w["merged"] = w["merged"] + (_sigmoid(z.pop("gm1"))
                                         * _dot(w.pop("yb").astype(BF16), w_ob_ref[0]))

        def log_decay():
            pre = _dot(z.pop("lr").astype(BF16), wa2) + ba_ref[0]
            log_sig = jnp.minimum(pre, 0.0) - jnp.log(1.0 + jnp.exp2(jnp.abs(pre) * (-LOG2E)))
            la_ref[rows, :] = log_sig * (LOG2E / GLA_NORMALIZER)

        def cumulative(ci):
            def f():
                la = la_ref[ci * C:(ci + 1) * C, :]
                p0 = la.astype(BF16)
                p1 = (la - p0.astype(F32)).astype(BF16)
                w["b", ci] = _dot(tri, p0) + _dot(tri, p1)
            return f

        def operands(ci):
            def f():
                b_all = w.pop(("b", ci))
                for h in range(GLA_HEADS):
                    ks = slice(h * GLA_DK, (h + 1) * GLA_DK)
                    b = b_all[:, ks]
                    qh = q_ref[ci * C:(ci + 1) * C, ks]
                    kh = k_ref[ci * C:(ci + 1) * C, ks]
                    ends = [b[(i + 1) * GLA_SUB - 1:(i + 1) * GLA_SUB, :] for i in range(nb)]
                    starts = [None] + ends[:-1]
                    ref_start = jnp.concatenate([jnp.zeros((GLA_SUB, GLA_DK), F32)]
                                                + [rows16(e) for e in ends[:-1]], axis=0)
                    ref_end = jnp.concatenate([rows16(e) for e in ends], axis=0)
                    rel = b - ref_start
                    qd = qh * jnp.exp2(rel)
                    kd = kh * jnp.exp2(-rel)
                    ke = kh * jnp.exp2(ref_end - b)

                    def q_from(i, j):
                        if j == i - 1:
                            return blk(qd, i)
                        return blk(qd, i) * rows16(jnp.exp2(starts[i] - ends[j]))

                    def k_to(i, j):
                        if j == i:
                            return blk(ke, i)
                        return blk(ke, i) * rows16(jnp.exp2(ends[j] - ends[i]))

                    pairs = [(qd.astype(BF16), kd.astype(BF16))]
                    for (sz, _) in levels:
                        qblocks, kblocks = [], []
                        for i in range(nb):
                            mid = (i // (2 * sz)) * 2 * sz + sz
                            if (i // sz) % 2 == 1:
                                qblocks.append(q_from(i, mid - 1))
                                kblocks.append(None)
                            else:
                                qblocks.append(None)
                                kblocks.append(k_to(i, mid - 1))
                        pairs.append((stack(qblocks), stack(kblocks)))
                    q_in = stack([blk(qd, 0)]
                                 + [blk(qd, i) * rows16(jnp.exp2(starts[i])) for i in range(1, nb)])
                    k_dec = stack([k_to(i, nb - 1) for i in range(nb)])
                    w["ops", ci, h] = (q_in, k_dec, jnp.exp2(ends[nb - 1]), pairs)
            return f

        def attention(ci):
            def f():
                for h in range(GLA_HEADS):
                    q_in, k_dec, dec, pairs = w.pop(("ops", ci, h))
                    att = jnp.where(mask_diag, _dot_nt(*pairs[0]), 0.0)
                    for (_, same_blk), pr in zip(levels, pairs[1:]):
                        part = _dot_nt(*pr)
                        att = att + (part if same_blk is None else jnp.where(same_blk, part, 0.0))
                    w["att", ci, h] = (q_in, k_dec, dec, att.astype(BF16))
            return f

        def state_chain(ci):
            def f():
                bi = 0 if BB == 1 else ci // chunks_per_seq
                for h in range(GLA_HEADS):
                    q_in, k_dec, dec, att = w.pop(("att", ci, h))
                    vs = slice(h * GLA_DV, (h + 1) * GLA_DV)
                    vh = v_ref[ci * C:(ci + 1) * C, vs]
                    st = st_ref[bi, h]
                    o_ref[ci * C:(ci + 1) * C, vs] = _dot(att, vh) + _dot(q_in, st.astype(BF16))
                    dec_col = jnp.broadcast_to(dec, (GLA_DK, GLA_DK)).T
                    dec_col = jnp.concatenate([dec_col] * (GLA_DV // GLA_DK), axis=1)
                    st_ref[bi, h] = st * dec_col + _dot_tn(k_dec, vh)
            return f

        def merge_c():
            parts = [_rms(o_ref[rows, h * GLA_DV:(h + 1) * GLA_DV], gla_g) for h in range(GLA_HEADS)]
            y_c = jnp.concatenate(parts, axis=-1) * _silu(z.pop("gc"))
            w["merged"] = w["merged"] + (_sigmoid(z.pop("gm2"))
                                         * _dot(y_c.astype(BF16), w_oc_ref[0]))

        def out_proj():
            w["out"] = _dot(w.pop("merged").astype(BF16), w_out_ref[0])

        def residual():
            out = w.pop("out")
            for idx, (bi, s0, n) in enumerate(segs):
                gate = mod_ref[0, bi, 2:3, :]
                y_ref[bi, s0:s0 + n, :] = (x_ref[bi, s0:s0 + n, :]
                                           + _rms(out[idx * n:(idx + 1) * n], post_g * gate))

        early = [pool, merge_a, sgu_norm, sgu_mix, merge_b, log_decay]
        early += [cumulative(ci) for ci in chunk_ids]
        early += [operands(ci) for ci in chunk_ids]
        late = [attention(ci) for ci in chunk_ids]
        late += [state_chain(ci) for ci in chunk_ids]
        late += [merge_c, out_proj, residual]
        return early, late

    zs = [dict() for _ in range(NSUB)]
    proj = [projection_stages(sub, zs[sub]) for sub in range(NSUB)]
    branch = [branch_stages(sub, zs[sub]) for sub in range(NSUB)]
    _interleave(proj[0][0])
    for r in range(NSUB + 1):
        matmul_side = ((proj[r - 1][2] if r >= 1 else []) + (proj[r][1] if r < NSUB else [])
                       + (proj[r + 1][0] if r + 1 < NSUB else []))
        _interleave(branch[r - 1][1] if r >= 1 else [], branch[r][0] if r < NSUB else [], matmul_side)

    for bi in range(BB):
        tail = ext_ref[bi, TT:TT + HIST_ROWS, :]
        ext_ref[bi, 0:HIST_ROWS, :] = tail
        hist_ref[bi] = tail

    @pl.when(t == nt - 1)
    def _():
        sfin_ref[...] = st_ref[...]


def _layer_tiles(B, T):
    if T >= SUBTILES_PER_STEP * SUBTILE_ROWS:
        BB, TT, NSUB = 1, SUBTILES_PER_STEP * SUBTILE_ROWS, SUBTILES_PER_STEP
    else:
        BB, TT, NSUB = min(B, SUBTILE_ROWS // T), T, 1
    L = min(T, SGU_LEN)
    C = min(GLA_CHUNK, TT)
    R = BB * TT // NSUB
    assert T % TT == 0 and B % BB == 0 and TT >= HIST_ROWS
    assert (NSUB == 1 or BB == 1) and R % L == 0 and R % C == 0 and TT % C == 0
    return BB, TT, NSUB, L, C


def _vmem_limit(BB, TT, NSUB, weights, has_state):
    M = BB * TT
    ring = min(NSUB, SUBTILE_SLOTS) * (M // NSUB)
    wbytes = sum(math.prod(w.shape[1:]) * w.dtype.itemsize for w in weights)
    io = 2 * 2 * M * D_MODEL * 4
    state = 2 * (2 if has_state else 1) * BB * (GLA_HEADS * GLA_DK * GLA_DV + HIST_ROWS * D_A) * 4
    scratch = ring * (D_MODEL * 2 + 3 * D_CK * 4 + D_CV * 2 + D_CV * 4) + BB * (TT + HIST_ROWS) * D_A * 4 \
        + BB * GLA_HEADS * GLA_DK * GLA_DV * 4
    temps = ring * (3 * D_MODEL + 6 * D_MODEL) * 4
    return min(wbytes + io + state + scratch + temps, V7X_VMEM_BYTES - VMEM_RESERVE_BYTES)


def _mixer_layer(l, x, mod, mod_row0, state, weights, pos0, emit_vn):
    B, T, D = x.shape
    BB, TT, NSUB, L, C = _layer_tiles(B, T)
    ring = min(NSUB, SUBTILE_SLOTS) * (BB * TT // NSUB)
    has_state = state is not None
    mod_blk0 = mod_row0 // BB

    def wspec(w):
        nd = w.ndim
        return pl.BlockSpec((1,) + w.shape[1:], lambda b, t: (l,) + (0,) * (nd - 1),
                            pipeline_mode=pl.Buffered(1))

    grid = (B // BB, T // TT)
    inputs = [x, mod]
    in_specs = [
        pl.BlockSpec((BB, TT, D), lambda b, t: (b, t, 0)),
        pl.BlockSpec((1, BB, 3, D), lambda b, t: (l, mod_blk0 + b, 0, 0)),
    ]
    if has_state:
        inputs += list(state)
        in_specs += [
            pl.BlockSpec((1, BB, HIST_ROWS, D_A), lambda b, t: (l, b, 0, 0)),
            pl.BlockSpec((1, BB, GLA_HEADS, GLA_DK, GLA_DV), lambda b, t: (l, b, 0, 0, 0)),
        ]
    inputs += list(weights)
    in_specs += [wspec(w) for w in weights]
    out_shape = [
        jax.ShapeDtypeStruct((B, T, D), F32),
        jax.ShapeDtypeStruct((B, HIST_ROWS, D_A), F32),
        jax.ShapeDtypeStruct((B, GLA_HEADS, GLA_DK, GLA_DV), F32),
    ]
    out_specs = [
        pl.BlockSpec((BB, TT, D), lambda b, t: (b, t, 0)),
        pl.BlockSpec((BB, HIST_ROWS, D_A), lambda b, t: (b, 0, 0)),
        pl.BlockSpec((BB, GLA_HEADS, GLA_DK, GLA_DV), lambda b, t: (b, 0, 0, 0)),
    ]
    if emit_vn:
        out_shape.append(jax.ShapeDtypeStruct((B, T, D_B), F32))
        out_specs.append(pl.BlockSpec((BB, TT, D_B), lambda b, t: (b, t, 0)))
    scratch_shapes = [
        pltpu.VMEM((ring, D), BF16),
        pltpu.VMEM((BB, HIST_ROWS + TT, D_A), F32),
        pltpu.VMEM((ring, D_CK), F32),
        pltpu.VMEM((ring, D_CK), F32),
        pltpu.VMEM((ring, D_CV), BF16),
        pltpu.VMEM((ring, D_CK), F32),
        pltpu.VMEM((ring, D_CV), F32),
        pltpu.VMEM((BB, GLA_HEADS, GLA_DK, GLA_DV), F32),
    ]
    body = functools.partial(_layer_kernel, BB=BB, TT=TT, NSUB=NSUB, L=L, C=C, pos0=pos0,
                             has_state=has_state, emit_vn=emit_vn)
    return pl.pallas_call(
        body,
        grid=grid,
        in_specs=in_specs,
        out_specs=out_specs,
        out_shape=out_shape,
        scratch_shapes=scratch_shapes,
        compiler_params=pltpu.CompilerParams(
            dimension_semantics=("arbitrary", "arbitrary"),
            vmem_limit_bytes=_vmem_limit(BB, TT, NSUB, weights, has_state)),
        name="mixer_layer_T%d" % T,
    )(*inputs)


def kernel(x_prompt, x_sample, state_pool, state_gla, c_prompt, c_sample, ada_w, ada_b, pre_norm_g,
           post_norm_g, w_in, pool_w, pool_scale, sgu_norm_g, sgu_w, sgu_b, gla_wa2, gla_ba, gla_norm_g,
           w_oa, w_ob, w_oc, w_out):
    bp = x_prompt.shape[0]
    bs = x_sample.shape[0]
    mod = _ada_modulation(jnp.concatenate([c_prompt, c_sample], axis=0), ada_w, ada_b)
    mod = mod.reshape(DEPTH, bp + bs, 3, D_MODEL)

    w_pack = jnp.swapaxes(w_in, 1, 2).astype(BF16)
    sgu_bias = jnp.repeat(jnp.swapaxes(sgu_b, 1, 2), SGU_GW, axis=2)
    vec = lambda v: v[:, None, :]
    weights = [vec(pre_norm_g), vec(post_norm_g), w_pack, pool_w, vec(pool_scale), vec(sgu_norm_g),
               sgu_w, sgu_bias, gla_wa2, vec(gla_ba), vec(gla_norm_g),
               w_oa.astype(BF16), w_ob.astype(BF16), w_oc.astype(BF16), w_out.astype(BF16)]
    hist_sample = jnp.pad(state_pool, ((0, 0), (0, 0), (HIST_ROWS - POOL_HIST, 0), (0, 0)))

    xp, xs = x_prompt, x_sample
    pool_p, gla_p, pool_s, gla_s, sgu_s = [], [], [], [], []
    for l in range(DEPTH):
        xp, hp, sp = _mixer_layer(l, xp, mod, 0, None, weights, 0, False)
        xs, hs, ss, vs = _mixer_layer(l, xs, mod, bp, (hist_sample, state_gla), weights, PAST_LEN, True)
        pool_p.append(hp[:, HIST_ROWS - POOL_HIST:])
        gla_p.append(sp)
        pool_s.append(hs[:, HIST_ROWS - POOL_HIST:])
        gla_s.append(ss)
        sgu_s.append(vs)
    return (xp, xs, jnp.stack(pool_p), jnp.stack(gla_p), jnp.stack(pool_s), jnp.stack(gla_s),
            jnp.stack(sgu_s))
```

```python
import functools
import math

import jax
import jax.numpy as jnp
from jax import lax
from jax.experimental import pallas as pl
from jax.experimental.pallas import tpu as pltpu

D_MODEL = 1024
DEPTH = 2
PAST_LEN = 2048
EPS = 1e-6
POOL_WINDOWS = (2, 4, 8, 16)
POOL_GW = 128
D_A = 512
POOL_HIST = 15
HIST_ROWS = 16
SGU_LEN = 128
SGU_GROUPS = 4
D_B = 512
SGU_GW = 128
GLA_HEADS = 4
GLA_DK = 128
GLA_DV = 256
D_CK = 512
D_CV = 1024
GLA_RANK = 16
GLA_NORMALIZER = 16.0
GLA_SUB = 16
GLA_CHUNK = 128
SUBTILE_ROWS = 256
SUBTILES_PER_STEP = 2
SUBTILE_SLOTS = 2
ADA_COL_BLOCKS = 3
LANE = 128
V7X_VMEM_BYTES = 64 * 1024 * 1024
VMEM_RESERVE_BYTES = 4 * 1024 * 1024
LOG2E = math.log2(math.e)

OFF_A, OFF_GA, OFF_U, OFF_VB, OFF_GB = 0, 512, 1024, 1536, 2048
OFF_Q, OFF_K, OFF_VC, OFF_GC = 2560, 3072, 3584, 4608
OFF_LR = 5632
OFF_GM = OFF_LR + GLA_RANK

F32 = jnp.float32
BF16 = jnp.bfloat16


def _dot(a, b):
    return jnp.dot(a, b, preferred_element_type=F32)


def _dot_nt(a, b):
    return lax.dot_general(a, b, (((1,), (1,)), ((), ())), preferred_element_type=F32)


def _dot_tn(a, b):
    return lax.dot_general(a, b, (((0,), (0,)), ((), ())), preferred_element_type=F32)


def _sigmoid(x):
    return 1.0 / (1.0 + jnp.exp2(x * (-LOG2E)))


def _silu(x):
    return x * _sigmoid(x)


def _rms(x, g):
    return x * lax.rsqrt(jnp.mean(x * x, axis=-1, keepdims=True) + EPS) * g


def _ada_kernel(c_ref, w_ref, b_ref, o_ref):
    c = c_ref[...]
    o_ref[0] = _dot(_silu(c).astype(BF16), w_ref[0].astype(BF16)) + b_ref[0]


def _ada_modulation(c_all, ada_w, ada_b):
    n = c_all.shape[0]
    return pl.pallas_call(
        _ada_kernel,
        grid=(DEPTH, ADA_COL_BLOCKS),
        in_specs=[
            pl.BlockSpec((n, D_MODEL), lambda l, j: (0, 0)),
            pl.BlockSpec((1, D_MODEL, D_MODEL), lambda l, j: (l, 0, j)),
            pl.BlockSpec((1, 1, D_MODEL), lambda l, j: (l, 0, j)),
        ],
        out_specs=pl.BlockSpec((1, n, D_MODEL), lambda l, j: (l, 0, j)),
        out_shape=jax.ShapeDtypeStruct((DEPTH, n, 3 * D_MODEL), F32),
        name="ada_modulation",
    )(c_all, ada_w, ada_b.reshape(DEPTH, 1, 3 * D_MODEL))


def _interleave(*stage_lists):
    lists = [sl for sl in stage_lists if sl]
    done = [0] * len(lists)
    while any(d < len(sl) for d, sl in zip(done, lists)):
        k = min((i for i in range(len(lists)) if done[i] < len(lists[i])),
                key=lambda i: done[i] / len(lists[i]))
        lists[k][done[k]]()
        done[k] += 1


def _layer_kernel(*refs, BB, TT, NSUB, L, C, pos0, has_state, emit_vn):
    refs = list(refs)
    x_ref, mod_ref = refs[:2]
    del refs[:2]
    if has_state:
        hist0_ref, s0_ref = refs[:2]
        del refs[:2]
    (pre_g_ref, post_g_ref, w_in_ref, pool_w_ref, pool_scale_ref, sgu_g_ref, sgu_w_ref, sgu_b_ref,
     wa2_ref, ba_ref, gla_g_ref, w_oa_ref, w_ob_ref, w_oc_ref, w_out_ref) = refs[:15]
    del refs[:15]
    y_ref, hist_ref, sfin_ref = refs[:3]
    del refs[:3]
    if emit_vn:
        vn_ref = refs.pop(0)
    hb_ref, ext_ref, q_ref, k_ref, v_ref, la_ref, o_ref, st_ref = refs

    M = BB * TT
    R = M // NSUB
    SLOTS = min(NSUB, SUBTILE_SLOTS)
    t = pl.program_id(1)
    nt = pl.num_programs(1)

    @pl.when(t == 0)
    def _():
        if has_state:
            ext_ref[:, 0:HIST_ROWS, :] = hist0_ref[0]
            st_ref[...] = s0_ref[0]
        else:
            ext_ref[:, 0:HIST_ROWS, :] = jnp.zeros((BB, HIST_ROWS, D_A), F32)
            st_ref[...] = jnp.zeros(st_ref.shape, F32)

    pre_g = pre_g_ref[0]
    post_g = post_g_ref[0]
    gla_g = gla_g_ref[0]
    ri = lax.broadcasted_iota(jnp.int32, (L, L), 0)
    ci_ = lax.broadcasted_iota(jnp.int32, (L, L), 1)
    wms = [jnp.where(ri >= ci_, sgu_w_ref[0, g, 0:L, 0:L], 0.0).astype(BF16) for g in range(SGU_GROUPS)]
    sgu_bias = sgu_b_ref[0, 0:L, :]
    wa2 = jnp.concatenate([wa2_ref[0], jnp.zeros((LANE - GLA_RANK, D_CK), F32)], axis=0).astype(BF16)
    cr = lax.broadcasted_iota(jnp.int32, (C, C), 0)
    cc = lax.broadcasted_iota(jnp.int32, (C, C), 1)
    tri = (cr >= cc).astype(BF16)
    sub_shift = GLA_SUB.bit_length() - 1
    mask_diag = (cr >= cc) & ((cr >> sub_shift) == (cc >> sub_shift))
    nb = C // GLA_SUB
    levels = []
    s = 1
    while 2 * s <= nb:
        sh = sub_shift + s.bit_length()
        levels.append((s, None if 2 * s == nb else (cr >> sh) == (cc >> sh)))
        s *= 2
    chunks_per_seq = TT // C

    def blk(x, i):
        return x[i * GLA_SUB:(i + 1) * GLA_SUB]

    def rows16(r):
        return jnp.broadcast_to(r, (GLA_SUB, r.shape[1]))

    def stack(blocks):
        zero = jnp.zeros((GLA_SUB, GLA_DK), F32)
        return jnp.concatenate([zero if b is None else b for b in blocks], axis=0).astype(BF16)

    def segments(sub):
        if BB == 1:
            return [(0, sub * R, R)]
        return [(bi, 0, TT) for bi in range(BB)]

    def projection_stages(sub, z):
        g0 = (sub % SLOTS) * R
        rows = slice(g0, g0 + R)

        def norm():
            for idx, (bi, s0, n) in enumerate(segments(sub)):
                xb = x_ref[bi, s0:s0 + n, :]
                shift = mod_ref[0, bi, 0:1, :]
                scale = mod_ref[0, bi, 1:2, :]
                hmod = _rms(xb, pre_g * (1.0 + scale)) + shift
                hb_ref[g0 + idx * n:g0 + (idx + 1) * n, :] = hmod.astype(BF16)

        def proj(off, width):
            return _dot_nt(w_in_ref[0, off:off + width, :], hb_ref[rows, :]).T

        def keep(name, off, width):
            def f():
                z[name] = proj(off, width)
            return f

        def put_a(a):
            z["a"] = a
            for idx, (bi, s0, n) in enumerate(segments(sub)):
                ext_ref[bi, HIST_ROWS + s0:HIST_ROWS + s0 + n, :] = a[idx * n:(idx + 1) * n]

        def put_q(q):
            q_ref[rows, :] = q * (GLA_DK ** -0.5)

        def put_k(k):
            k_ref[rows, :] = k

        def put(name):
            return lambda val: z.__setitem__(name, val)

        def proj_v():
            v_ref[rows, :] = proj(OFF_VC, D_CV).astype(BF16)

        def joint(*pieces):
            lo = pieces[0][1]
            hi = max(off + width for _, off, width in pieces)

            def f():
                zt = _dot_nt(w_in_ref[0, lo:hi, :], hb_ref[rows, :])
                for consume, off, width in pieces:
                    consume(zt[off - lo:off - lo + width, :].T)
            return f

        early = [norm,
                 joint((put_a, OFF_A, D_A), (put("ga"), OFF_GA, D_A)),
                 joint((put("lr"), OFF_LR, LANE), (put("gm0"), OFF_GM, D_MODEL)),
                 joint((put("u"), OFF_U, D_B), (put("vb"), OFF_VB, D_B)),
                 joint((put("gb"), OFF_GB, D_B), (put_q, OFF_Q, D_CK)),
                 keep("gm1", OFF_GM + D_MODEL, D_MODEL),
                 joint((put_k, OFF_K, D_CK))]
        last = [keep("gc", OFF_GC, D_CV), keep("gm2", OFF_GM + 2 * D_MODEL, D_MODEL)]
        return early, [proj_v], last

    def branch_stages(sub, z):
        g0 = (sub % SLOTS) * R
        rows = slice(g0, g0 + R)
        segs = segments(sub)
        c_first = g0 // C
        chunk_ids = list(range(c_first, c_first + R // C))
        w = {}

        def pool():
            parts = []
            for idx, (bi, s0, n) in enumerate(segs):
                ab = z["a"][idx * n:(idx + 1) * n]
                row = lax.broadcasted_iota(jnp.int32, (n, POOL_GW), 0)
                pos1 = row + (t * TT + s0 + pos0 + 1)
                groups = []
                for gi, win in enumerate(POOL_WINDOWS):
                    ls = slice(gi * POOL_GW, (gi + 1) * POOL_GW)
                    base = HIST_ROWS + s0
                    terms = [ext_ref[bi, base - k:base - k + n, ls] for k in range(win)]
                    while len(terms) > 1:
                        terms = [terms[i] + terms[i + 1] for i in range(0, len(terms), 2)]
                    cnt = jnp.minimum(pos1, win).astype(F32)
                    dgi = terms[0] / cnt - ab[:, ls]
                    groups.append(_dot(dgi.astype(BF16), pool_w_ref[0, gi].astype(BF16)))
                parts.append(jnp.concatenate(groups, axis=-1))
            w["ya"] = parts[0] if len(parts) == 1 else jnp.concatenate(parts, axis=0)

        def merge_a():
            y_a = w.pop("ya") * pool_scale_ref[0] * _silu(z.pop("ga"))
            w["merged"] = (_sigmoid(z.pop("gm0"))
                           * _dot(y_a.astype(BF16), w_oa_ref[0]))

        def sgu_norm():
            vb = z.pop("vb")
            mu = jnp.mean(vb, axis=-1, keepdims=True)
            vc = vb - mu
            vn = vc * lax.rsqrt(jnp.mean(vc * vc, axis=-1, keepdims=True) + EPS) * sgu_g_ref[0]
            if emit_vn:
                for idx, (bi, s0, n) in enumerate(segs):
                    vn_ref[bi, s0:s0 + n, :] = vn[idx * n:(idx + 1) * n]
            w["vnb"] = vn.astype(BF16)

        def sgu_mix():
            vnb = w.pop("vnb")
            s_rows = []
            for c0 in range(0, R, L):
                parts = [_dot(wms[g], vnb[c0:c0 + L, g * SGU_GW:(g + 1) * SGU_GW])
                         for g in range(SGU_GROUPS)]
                s_rows.append(jnp.concatenate(parts, axis=-1) + sgu_bias)
            s_all = s_rows[0] if len(s_rows) == 1 else jnp.concatenate(s_rows, axis=0)
            w["yb"] = z.pop("u") * s_all * _silu(z.pop("gb"))

        def merge_b():
            w["merged"] = w["merged"] + (_sigmoid(z.pop("gm1"))
                                         * _dot(w.pop("yb").astype(BF16), w_ob_ref[0]))

        def log_decay():
            pre = _dot(z.pop("lr").astype(BF16), wa2) + ba_ref[0]
            log_sig = jnp.minimum(pre, 0.0) - jnp.log(1.0 + jnp.exp2(jnp.abs(pre) * (-LOG2E)))
            la_ref[rows, :] = log_sig * (LOG2E / GLA_NORMALIZER)

        def cumulative(ci):
            def f():
                la = la_ref[ci * C:(ci + 1) * C, :]
                p0 = la.astype(BF16)
                p1 = (la - p0.astype(F32)).astype(BF16)
                w["b", ci] = _dot(tri, p0) + _dot(tri, p1)
            return f

        def operands(ci):
            def f():
                b_all = w.pop(("b", ci))
                for h in range(GLA_HEADS):
                    ks = slice(h * GLA_DK, (h + 1) * GLA_DK)
                    b = b_all[:, ks]
                    qh = q_ref[ci * C:(ci + 1) * C, ks]
                    kh = k_ref[ci * C:(ci + 1) * C, ks]
                    ends = [b[(i + 1) * GLA_SUB - 1:(i + 1) * GLA_SUB, :] for i in range(nb)]
                    starts = [None] + ends[:-1]
                    ref_start = jnp.concatenate([jnp.zeros((GLA_SUB, GLA_DK), F32)]
                                                + [rows16(e) for e in ends[:-1]], axis=0)
                    ref_end = jnp.concatenate([rows16(e) for e in ends], axis=0)
                    rel = b - ref_start
                    qd = qh * jnp.exp2(rel)
                    kd = kh * jnp.exp2(-rel)
                    ke = kh * jnp.exp2(ref_end - b)

                    def q_from(i, j):
                        if j == i - 1:
                            return blk(qd, i)
                        return blk(qd, i) * rows16(jnp.exp2(starts[i] - ends[j]))

                    def k_to(i, j):
                        if j == i:
                            return blk(ke, i)
                        return blk(ke, i) * rows16(jnp.exp2(ends[j] - ends[i]))

                    pairs = [(qd.astype(BF16), kd.astype(BF16))]
                    for (sz, _) in levels:
                        qblocks, kblocks = [], []
                        for i in range(nb):
                            mid = (i // (2 * sz)) * 2 * sz + sz
                            if (i // sz) % 2 == 1:
                                qblocks.append(q_from(i, mid - 1))
                                kblocks.append(None)
                            else:
                                qblocks.append(None)
                                kblocks.append(k_to(i, mid - 1))
                        pairs.append((stack(qblocks), stack(kblocks)))
                    q_in = stack([blk(qd, 0)]
                                 + [blk(qd, i) * rows16(jnp.exp2(starts[i])) for i in range(1, nb)])
                    k_dec = stack([k_to(i, nb - 1) for i in range(nb)])
                    w["ops", ci, h] = (q_in, k_dec, jnp.exp2(ends[nb - 1]), pairs)
            return f

        def attention(ci):
            def f():
                for h in range(GLA_HEADS):
                    q_in, k_dec, dec, pairs = w.pop(("ops", ci, h))
                    att = jnp.where(mask_diag, _dot_nt(*pairs[0]), 0.0)
                    for (_, same_blk), pr in zip(levels, pairs[1:]):
                        part = _dot_nt(*pr)
                        att = att + (part if same_blk is None else jnp.where(same_blk, part, 0.0))
                    w["att", ci, h] = (q_in, k_dec, dec, att.astype(BF16))
            return f

        def state_chain(ci):
            def f():
                bi = 0 if BB == 1 else ci // chunks_per_seq
                for h in range(GLA_HEADS):
                    q_in, k_dec, dec, att = w.pop(("att", ci, h))
                    vs = slice(h * GLA_DV, (h + 1) * GLA_DV)
                    vh = v_ref[ci * C:(ci + 1) * C, vs]
                    st = st_ref[bi, h]
                    o_ref[ci * C:(ci + 1) * C, vs] = _dot(att, vh) + _dot(q_in, st.astype(BF16))
                    dec_col = jnp.broadcast_to(dec, (GLA_DK, GLA_DK)).T
                    dec_col = jnp.concatenate([dec_col] * (GLA_DV // GLA_DK), axis=1)
                    st_ref[bi, h] = st * dec_col + _dot_tn(k_dec, vh)
            return f

        def merge_c():
            parts = [_rms(o_ref[rows, h * GLA_DV:(h + 1) * GLA_DV], gla_g) for h in range(GLA_HEADS)]
            y_c = jnp.concatenate(parts, axis=-1) * _silu(z.pop("gc"))
            w["merged"] = w["merged"] + (_sigmoid(z.pop("gm2"))
                                         * _dot(y_c.astype(BF16), w_oc_ref[0]))

        def out_proj():
            w["out"] = _dot(w.pop("merged").astype(BF16), w_out_ref[0])

        def residual():
            out = w.pop("out")
            for idx, (bi, s0, n) in enumerate(segs):
                gate = mod_ref[0, bi, 2:3, :]
                y_ref[bi, s0:s0 + n, :] = (x_ref[bi, s0:s0 + n, :]
                                           + _rms(out[idx * n:(idx + 1) * n], post_g * gate))

        early = [pool, merge_a, sgu_norm, sgu_mix, merge_b, log_decay]
        early += [cumulative(ci) for ci in chunk_ids]
        early += [operands(ci) for ci in chunk_ids]
        late = [attention(ci) for ci in chunk_ids]
        late += [state_chain(ci) for ci in chunk_ids]
        late += [merge_c, out_proj, residual]
        return early, late

    zs = [dict() for _ in range(NSUB)]
    proj = [projection_stages(sub, zs[sub]) for sub in range(NSUB)]
    branch = [branch_stages(sub, zs[sub]) for sub in range(NSUB)]
    _interleave(proj[0][0])
    for r in range(NSUB + 1):
        matmul_side = ((proj[r - 1][2] if r >= 1 else []) + (proj[r][1] if r < NSUB else [])
                       + (proj[r + 1][0] if r + 1 < NSUB else []))
        _interleave(branch[r - 1][1] if r >= 1 else [], branch[r][0] if r < NSUB else [], matmul_side)

    for bi in range(BB):
        tail = ext_ref[bi, TT:TT + HIST_ROWS, :]
        ext_ref[bi, 0:HIST_ROWS, :] = tail
        hist_ref[bi] = tail

    @pl.when(t == nt - 1)
    def _():
        sfin_ref[...] = st_ref[...]


def _layer_tiles(B, T):
    if T >= SUBTILES_PER_STEP * SUBTILE_ROWS:
        BB, TT, NSUB = 1, SUBTILES_PER_STEP * SUBTILE_ROWS, SUBTILES_PER_STEP
    else:
        BB, TT, NSUB = min(B, SUBTILE_ROWS // T), T, 1
    L = min(T, SGU_LEN)
    C = min(GLA_CHUNK, TT)
    R = BB * TT // NSUB
    assert T % TT == 0 and B % BB == 0 and TT >= HIST_ROWS
    assert (NSUB == 1 or BB == 1) and R % L == 0 and R % C == 0 and TT % C == 0
    return BB, TT, NSUB, L, C


def _vmem_limit(BB, TT, NSUB, weights, has_state):
    M = BB * TT
    ring = min(NSUB, SUBTILE_SLOTS) * (M // NSUB)
    wbytes = sum(math.prod(w.shape[1:]) * w.dtype.itemsize for w in weights)
    io = 2 * 2 * M * D_MODEL * 4
    state = 2 * (2 if has_state else 1) * BB * (GLA_HEADS * GLA_DK * GLA_DV + HIST_ROWS * D_A) * 4
    scratch = ring * (D_MODEL * 2 + 3 * D_CK * 4 + D_CV * 2 + D_CV * 4) + BB * (TT + HIST_ROWS) * D_A * 4 \
        + BB * GLA_HEADS * GLA_DK * GLA_DV * 4
    temps = ring * (3 * D_MODEL + 6 * D_MODEL) * 4
    return min(wbytes + io + state + scratch + temps, V7X_VMEM_BYTES - VMEM_RESERVE_BYTES)


def _mixer_layer(l, x, mod, mod_row0, state, weights, pos0, emit_vn):
    B, T, D = x.shape
    BB, TT, NSUB, L, C = _layer_tiles(B, T)
    ring = min(NSUB, SUBTILE_SLOTS) * (BB * TT // NSUB)
    has_state = state is not None
    mod_blk0 = mod_row0 // BB

    def wspec(w):
        nd = w.ndim
        return pl.BlockSpec((1,) + w.shape[1:], lambda b, t: (l,) + (0,) * (nd - 1),
                            pipeline_mode=pl.Buffered(1))

    grid = (B // BB, T // TT)
    inputs = [x, mod]
    in_specs = [
        pl.BlockSpec((BB, TT, D), lambda b, t: (b, t, 0)),
        pl.BlockSpec((1, BB, 3, D), lambda b, t: (l, mod_blk0 + b, 0, 0)),
    ]
    if has_state:
        inputs += list(state)
        in_specs += [
            pl.BlockSpec((1, BB, HIST_ROWS, D_A), lambda b, t: (l, b, 0, 0)),
            pl.BlockSpec((1, BB, GLA_HEADS, GLA_DK, GLA_DV), lambda b, t: (l, b, 0, 0, 0)),
        ]
    inputs += list(weights)
    in_specs += [wspec(w) for w in weights]
    out_shape = [
        jax.ShapeDtypeStruct((B, T, D), F32),
        jax.ShapeDtypeStruct((B, HIST_ROWS, D_A), F32),
        jax.ShapeDtypeStruct((B, GLA_HEADS, GLA_DK, GLA_DV), F32),
    ]
    out_specs = [
        pl.BlockSpec((BB, TT, D), lambda b, t: (b, t, 0)),
        pl.BlockSpec((BB, HIST_ROWS, D_A), lambda b, t: (b, 0, 0)),
        pl.BlockSpec((BB, GLA_HEADS, GLA_DK, GLA_DV), lambda b, t: (b, 0, 0, 0)),
    ]
    if emit_vn:
        out_shape.append(jax.ShapeDtypeStruct((B, T, D_B), F32))
        out_specs.append(pl.BlockSpec((BB, TT, D_B), lambda b, t: (b, t, 0)))
    scratch_shapes = [
        pltpu.VMEM((ring, D), BF16),
        pltpu.VMEM((BB, HIST_ROWS + TT, D_A), F32),
        pltpu.VMEM((ring, D_CK), F32),
        pltpu.VMEM((ring, D_CK), F32),
        pltpu.VMEM((ring, D_CV), BF16),
        pltpu.VMEM((ring, D_CK), F32),
        pltpu.VMEM((ring, D_CV), F32),
        pltpu.VMEM((BB, GLA_HEADS, GLA_DK, GLA_DV), F32),
    ]
    body = functools.partial(_layer_kernel, BB=BB, TT=TT, NSUB=NSUB, L=L, C=C, pos0=pos0,
                             has_state=has_state, emit_vn=emit_vn)
    return pl.pallas_call(
        body,
        grid=grid,
        in_specs=in_specs,
        out_specs=out_specs,
        out_shape=out_shape,
        scratch_shapes=scratch_shapes,
        compiler_params=pltpu.CompilerParams(
            dimension_semantics=("arbitrary", "arbitrary"),
            vmem_limit_bytes=_vmem_limit(BB, TT, NSUB, weights, has_state)),
        name="mixer_layer_T%d" % T,
    )(*inputs)


def kernel(x_prompt, x_sample, state_pool, state_gla, c_prompt, c_sample, ada_w, ada_b, pre_norm_g,
           post_norm_g, w_in, pool_w, pool_scale, sgu_norm_g, sgu_w, sgu_b, gla_wa2, gla_ba, gla_norm_g,
           w_oa, w_ob, w_oc, w_out):
    bp = x_prompt.shape[0]
    bs = x_sample.shape[0]
    mod = _ada_modulation(jnp.concatenate([c_prompt, c_sample], axis=0), ada_w, ada_b)
    mod = mod.reshape(DEPTH, bp + bs, 3, D_MODEL)

    w_pack = jnp.swapaxes(w_in, 1, 2).astype(BF16)
    sgu_bias = jnp.repeat(jnp.swapaxes(sgu_b, 1, 2), SGU_GW, axis=2)
    vec = lambda v: v[:, None, :]
    weights = [vec(pre_norm_g), vec(post_norm_g), w_pack, pool_w, vec(pool_scale), vec(sgu_norm_g),
               sgu_w, sgu_bias, gla_wa2, vec(gla_ba), vec(gla_norm_g),
               w_oa.astype(BF16), w_ob.astype(BF16), w_oc.astype(BF16), w_out.astype(BF16)]
    hist_sample = jnp.pad(state_pool, ((0, 0), (0, 0), (HIST_ROWS - POOL_HIST, 0), (0, 0)))

    xp, xs = x_prompt, x_sample
    pool_p, gla_p, pool_s, gla_s, sgu_s = [], [], [], [], []
    for l in range(DEPTH):
        xp, hp, sp = _mixer_layer(l, xp, mod, 0, None, weights, 0, False)
        xs, hs, ss, vs = _mixer_layer(l, xs, mod, bp, (hist_sample, state_gla), weights, PAST_LEN, True)
        pool_p.append(hp[:, HIST_ROWS - POOL_HIST:])
        gla_p.append(sp)
        pool_s.append(hs[:, HIST_ROWS - POOL_HIST:])
        gla_s.append(ss)
        sgu_s.append(vs)
    return (xp, xs, jnp.stack(pool_p), jnp.stack(gla_p), jnp.stack(pool_s), jnp.stack(gla_s),
            jnp.stack(sgu_s))
```
